```python
import math
import numpy as np
import jax
import jax.numpy as jnp
from jax import lax

D_MODEL = 1024
BATCH = 4
SEQ = 8192
DEPTH = 2

CHUNK = 64
QBLOCK = 128
N_BRANCH = 4
BRANCH_WIDTH = D_MODEL // 2
HEAD_DIM = 64
FOX_HEADS = BRANCH_WIDTH // HEAD_DIM
SB_HEADS = BRANCH_WIDTH // HEAD_DIM
GDN_HEADS = BRANCH_WIDTH // HEAD_DIM
GDN_HEAD_DIM = HEAD_DIM
GDN_CONV = 4
GLA_HEADS = 4
GLA_VALUE_DIM = BRANCH_WIDTH // GLA_HEADS
GLA_KEY_DIM = GLA_VALUE_DIM // 2
GLA_GATE_RANK = 16
GLA_GATE_TAU = 16.0
NORM_EPS = 1e-6

IN_SPLITS = (
    3 * BRANCH_WIDTH, FOX_HEADS, BRANCH_WIDTH,
    3 * BRANCH_WIDTH, BRANCH_WIDTH,
    3 * BRANCH_WIDTH, GDN_HEADS, GDN_HEADS, BRANCH_WIDTH,
    2 * GLA_HEADS * GLA_KEY_DIM, BRANCH_WIDTH, GLA_GATE_RANK, BRANCH_WIDTH,
    N_BRANCH * D_MODEL,
)
N_IN = sum(IN_SPLITS)

kernel_name = 'hybrid_fox_sb_gdn_gla_gated_merge'


def rms_norm(x, g):
    xf = x.astype(jnp.float32)
    y = xf * lax.rsqrt(jnp.mean(xf * xf, axis=-1, keepdims=True) + NORM_EPS)
    return y.astype(x.dtype) * g


def l2_norm(x):
    xf = x.astype(jnp.float32)
    return (xf * lax.rsqrt(jnp.sum(xf * xf, axis=-1, keepdims=True) + NORM_EPS)).astype(x.dtype)


def split_cols(t, sizes):
    return jnp.split(t, np.cumsum(sizes)[:-1].tolist(), axis=-1)


def heads(t, n):
    b, s, _ = t.shape
    return t.reshape(b, s, n, -1).transpose(0, 2, 1, 3)


def merge_heads(o):
    b, h, s, d = o.shape
    return o.transpose(0, 2, 1, 3).reshape(b, s, h * d)


def sweep_query_blocks(fn, q, *per_query):
    b, h, s = q.shape[:3]
    nb = s // QBLOCK

    def blocks(t):
        return jnp.moveaxis(t.reshape(b, h, nb, QBLOCK, *t.shape[3:]), 2, 0)

    out = lax.map(fn, (jnp.arange(nb), blocks(q), *[blocks(t) for t in per_query]))
    return jnp.moveaxis(out, 0, 2).reshape(b, h, s, -1)


def forgetting_attention(q, k, v, log_f):
    s = q.shape[2]
    c = jnp.cumsum(log_f, axis=-1)
    kpos = jnp.arange(s)
    scale = q.shape[-1] ** -0.5

    def block(args):
        i, q_i, c_i = args
        qpos = i * QBLOCK + jnp.arange(QBLOCK)
        logits = jnp.einsum('bhqd,bhkd->bhqk', q_i, k, preferred_element_type=jnp.float32) * scale
        logits = logits + c_i[..., :, None] - c[..., None, :]
        logits = jnp.where(kpos[None, :] <= qpos[:, None], logits, -jnp.inf)
        p = jax.nn.softmax(logits, axis=-1)
        return jnp.einsum('bhqk,bhkd->bhqd', p.astype(v.dtype), v)

    return sweep_query_blocks(block, q, c)


def stick_breaking_attention(q, k, v):
    s = q.shape[2]
    kpos = jnp.arange(s)
    scale = q.shape[-1] ** -0.5

    def block(args):
        i, q_i = args
        qpos = i * QBLOCK + jnp.arange(QBLOCK)
        z = jnp.einsum('bhqd,bhkd->bhqk', q_i, k, preferred_element_type=jnp.float32) * scale
        strict = kpos[None, :] < qpos[:, None]
        log_beta = jax.nn.log_sigmoid(z)
        log_rest = jnp.where(strict, jax.nn.log_sigmoid(-z), 0.0)
        suffix = lax.cumsum(log_rest, axis=log_rest.ndim - 1, reverse=True) - log_rest
        w = jnp.where(strict, jnp.exp(log_beta + suffix), 0.0)
        return jnp.einsum('bhqk,bhkd->bhqd', w.astype(v.dtype), v)

    return sweep_query_blocks(block, q)


def causal_depthwise_conv(x, w):
    kw = w.shape[0]
    return lax.conv_general_dilated(
        x, w[:, None, :].astype(x.dtype), window_strides=(1,), padding=[(kw - 1, 0)],
        dimension_numbers=('NWC', 'WIO', 'NWC'), feature_group_count=x.shape[-1])


def to_chunks(t):
    b, h, s = t.shape[:3]
    return jnp.moveaxis(t.reshape(b, h, s // CHUNK, CHUNK, *t.shape[3:]), 2, 0)


def from_chunks(t):
    n, b, h, c = t.shape[:4]
    return jnp.moveaxis(t, 0, 2).reshape(b, h, n * c, *t.shape[4:])


def gated_delta_rule(q, k, v, g, beta):
    out_dtype = v.dtype
    f32 = jnp.float32
    q, k, v = [to_chunks(t.astype(f32)) for t in (q, k, v)]
    q = q * q.shape[-1] ** -0.5
    dv = v.shape[-1]
    gc = jnp.cumsum(to_chunks(g.astype(f32)), axis=-1)
    beta = to_chunks(beta.astype(f32))[..., None]
    causal = jnp.tril(jnp.ones((CHUNK, CHUNK), dtype=bool))
    strict = jnp.tril(jnp.ones((CHUNK, CHUNK), dtype=bool), -1)
    decay = jnp.exp(jnp.where(causal, gc[..., :, None] - gc[..., None, :], -jnp.inf))
    kb = k * beta
    a = jnp.einsum('nbhik,nbhjk->nbhij', kb, k) * decay
    lower = jnp.where(strict, a, 0.0) + jnp.eye(CHUNK, dtype=f32)
    rhs = jnp.concatenate([v * beta, kb * jnp.exp(gc)[..., None]], axis=-1)
    sol = lax.linalg.triangular_solve(lower, rhs, left_side=True, lower=True, unit_diagonal=True)
    u, w = sol[..., :dv], sol[..., dv:]
    intra = jnp.einsum('nbhik,nbhjk->nbhij', q, k) * decay
    q_dec = q * jnp.exp(gc)[..., None]
    k_dec = k * jnp.exp(gc[..., -1:] - gc)[..., None]
    last = jnp.exp(gc[..., -1])[..., None, None]

    def step(state, xs):
        u_n, w_n, q_n, k_n, a_n, l_n = xs
        v_new = u_n - jnp.einsum('bhck,bhkv->bhcv', w_n, state)
        o = jnp.einsum('bhck,bhkv->bhcv', q_n, state) + jnp.einsum('bhij,bhjv->bhiv', a_n, v_new)
        state = state * l_n + jnp.einsum('bhck,bhcv->bhkv', k_n, v_new)
        return state, o

    state0 = jnp.zeros(q.shape[1:3] + (q.shape[-1], dv), f32)
    _, o = lax.scan(step, state0, (u, w, q_dec, k_dec, intra, last))
    return from_chunks(o).astype(out_dtype)


def gla_chunked(q, k, v, log_a):
    out_dtype = v.dtype
    f32 = jnp.float32
    q, k, v, log_a = [to_chunks(t.astype(f32)) for t in (q, k, v, log_a)]
    q = q * q.shape[-1] ** -0.5
    bc = jnp.cumsum(log_a, axis=-2)
    causal = jnp.tril(jnp.ones((CHUNK, CHUNK), dtype=bool))[..., None]

    def step(state, xs):
        q_n, k_n, v_n, b_n = xs
        diff = b_n[..., :, None, :] - b_n[..., None, :, :]
        dec = jnp.exp(jnp.where(causal, diff, -jnp.inf))
        a = jnp.einsum('bhtk,bhsk,bhtsk->bhts', q_n, k_n, dec)
        o = jnp.einsum('bhts,bhsv->bhtv', a, v_n) + jnp.einsum('bhtk,bhkv->bhtv', q_n * jnp.exp(b_n), state)
        b_last = b_n[..., -1:, :]
        state = jnp.exp(b_last[..., 0, :])[..., None] * state + jnp.einsum(
            'bhsk,bhsv->bhkv', k_n * jnp.exp(b_last - b_n), v_n)
        return state, o

    state0 = jnp.zeros(q.shape[1:3] + (q.shape[-1], v.shape[-1]), f32)
    _, o = lax.scan(step, state0, (q, k, v, bc))
    return from_chunks(o).astype(out_dtype)


def hybrid_layer(x, norm_g, w_in, fox_f_bias, fox_q_norm, fox_k_norm, sb_q_norm, sb_k_norm,
                 gdn_conv_w, gdn_a_log, gdn_dt_bias, gdn_o_norm, gla_gate_w2, gla_gate_bias,
                 gla_o_norm, gate_bias, w_branch, w_out):
    b, s, _ = x.shape
    f32 = jnp.float32
    h = rms_norm(x, norm_g)
    (fox_qkv, fox_f, fox_z, sb_qkv, sb_z, gdn_qkv, gdn_a, gdn_b, gdn_z,
     gla_qk, gla_v, gla_g, gla_z, merge_logits) = split_cols(h @ w_in, IN_SPLITS)

    fq, fk, fv = [heads(t, FOX_HEADS) for t in jnp.split(fox_qkv, 3, axis=-1)]
    log_f = jax.nn.log_sigmoid(fox_f.astype(f32) + fox_f_bias).transpose(0, 2, 1)
    o_fox = forgetting_attention(rms_norm(fq, fox_q_norm), rms_norm(fk, fox_k_norm), fv, log_f)

    sq, sk, sv = [heads(t, SB_HEADS) for t in jnp.split(sb_qkv, 3, axis=-1)]
    o_sb = stick_breaking_attention(rms_norm(sq, sb_q_norm), rms_norm(sk, sb_k_norm), sv)

    gdn_qkv = jax.nn.silu(causal_depthwise_conv(gdn_qkv, gdn_conv_w))
    gq, gk, gv = [heads(t, GDN_HEADS) for t in jnp.split(gdn_qkv, 3, axis=-1)]
    g = (-jnp.exp(gdn_a_log) * jax.nn.softplus(gdn_a.astype(f32) + gdn_dt_bias)).transpose(0, 2, 1)
    beta = jax.nn.sigmoid(gdn_b.astype(f32)).transpose(0, 2, 1)
    o_gdn = rms_norm(gated_delta_rule(l2_norm(gq), l2_norm(gk), gv, g, beta), gdn_o_norm)

    lq, lk = [heads(t, GLA_HEADS) for t in jnp.split(gla_qk, 2, axis=-1)]
    lv = heads(gla_v, GLA_HEADS)
    log_a = jax.nn.log_sigmoid((gla_g @ gla_gate_w2 + gla_gate_bias).astype(f32)) / GLA_GATE_TAU
    o_gla = rms_norm(gla_chunked(lq, lk, lv, heads(log_a, GLA_HEADS)), gla_o_norm)

    branches = jnp.stack([merge_heads(o) for o in (o_fox, o_sb, o_gdn, o_gla)]) * jax.nn.silu(
        jnp.stack([fox_z, sb_z, gdn_z, gla_z]))
    proj = jnp.einsum('nbsw,nwd->nbsd', branches, w_branch)
    gates = jax.nn.sigmoid(merge_logits.reshape(b, s, N_BRANCH, D_MODEL) + gate_bias)
    merged = jnp.einsum('nbsd,bsnd->bsd', proj, gates)
    return x + merged @ w_out


def setup_inputs(seed: int = 0) -> dict:
    key = jax.random.key(seed)
    ks = jax.random.split(key, 18)

    def normal(k, shape, scale):
        return jax.random.normal(k, shape, jnp.float32) * scale

    def gain(k, n):
        return 1.0 + 0.02 * normal(k, (DEPTH, n), 1.0)

    dt = jnp.exp(jax.random.uniform(ks[10], (DEPTH, GDN_HEADS), jnp.float32,
                                    minval=math.log(1e-3), maxval=math.log(1e-1)))
    return {
        'x': normal(ks[0], (BATCH, SEQ, D_MODEL), 1.0),
        'norm_g': gain(ks[1], D_MODEL),
        'w_in': normal(ks[2], (DEPTH, D_MODEL, N_IN), D_MODEL ** -0.5),
        'fox_f_bias': 3.0 + 0.1 * normal(ks[3], (DEPTH, FOX_HEADS), 1.0),
        'fox_q_norm': gain(ks[4], HEAD_DIM),
        'fox_k_norm': gain(ks[5], HEAD_DIM),
        'sb_q_norm': gain(ks[6], HEAD_DIM),
        'sb_k_norm': gain(ks[7], HEAD_DIM),
        'gdn_conv_w': normal(ks[8], (DEPTH, GDN_CONV, 3 * BRANCH_WIDTH), GDN_CONV ** -0.5),
        'gdn_a_log': jnp.log(jax.random.uniform(ks[9], (DEPTH, GDN_HEADS), jnp.float32, minval=1.0, maxval=16.0)),
        'gdn_dt_bias': dt + jnp.log(-jnp.expm1(-dt)),
        'gdn_o_norm': gain(ks[11], GDN_HEAD_DIM),
        'gla_gate_w2': normal(ks[12], (DEPTH, GLA_GATE_RANK, GLA_HEADS * GLA_KEY_DIM), GLA_GATE_RANK ** -0.5),
        'gla_gate_bias': 0.1 * normal(ks[13], (DEPTH, GLA_HEADS * GLA_KEY_DIM), 1.0),
        'gla_o_norm': gain(ks[14], GLA_VALUE_DIM),
        'gate_bias': 0.1 * normal(ks[15], (DEPTH, N_BRANCH, D_MODEL), 1.0),
        'w_branch': normal(ks[16], (DEPTH, N_BRANCH, BRANCH_WIDTH, D_MODEL), BRANCH_WIDTH ** -0.5),
        'w_out': normal(ks[17], (DEPTH, D_MODEL, D_MODEL), D_MODEL ** -0.5),
    }


def reference(x, norm_g, w_in, fox_f_bias, fox_q_norm, fox_k_norm, sb_q_norm, sb_k_norm,
              gdn_conv_w, gdn_a_log, gdn_dt_bias, gdn_o_norm, gla_gate_w2, gla_gate_bias,
              gla_o_norm, gate_bias, w_branch, w_out):
    for l in range(DEPTH):
        x = hybrid_layer(x, norm_g[l], w_in[l], fox_f_bias[l], fox_q_norm[l], fox_k_norm[l],
                         sb_q_norm[l], sb_k_norm[l], gdn_conv_w[l], gdn_a_log[l], gdn_dt_bias[l],
                         gdn_o_norm[l], gla_gate_w2[l], gla_gate_bias[l], gla_o_norm[l],
                         gate_bias[l], w_branch[l], w_out[l])
    return x
```

```python
import functools

import numpy as np
import jax
import jax.numpy as jnp
from jax import lax
from jax.experimental import pallas as pl
from jax.experimental.pallas import tpu as pltpu

F32 = jnp.float32
BF16 = jnp.bfloat16

LANES = 128
HEAD_DIM = 64
D_MODEL = 1024
BRANCH_WIDTH = 512
N_BRANCH = 4
N_HEADS = 8
GLA_HEADS = 4
GLA_GATE_RANK = 16
GLA_GATE_TAU = 16.0
GDN_CONV = 4
CHUNK = 64
NORM_EPS = 1e-6
VMEM_LIMIT = 48 * 1024 * 1024
EXP_UNDERFLOW = 104.0
NEG_BIG = -1e30

C_MERGE, C_Z, C_GLA, C_GDN, C_FOX, C_SB, C_END = 0, 32, 48, 56, 68, 80, 92
N_BIG = C_END * LANES
L_FOXF, L_GDNA, L_GDNB, L_GLAG = 0, 8, 16, 24


def _dot(a, b):
    return jnp.dot(a, b, preferred_element_type=F32)


def _dot_nt(a, b):
    return lax.dot_general(a, b, (((1,), (1,)), ((), ())), preferred_element_type=F32)


def _dot_tn(a, b):
    return lax.dot_general(a, b, (((0,), (0,)), ((), ())), preferred_element_type=F32)


def _split2(x):
    hi = x.astype(BF16)
    lo = (x - hi.astype(F32)).astype(BF16)
    return hi, lo


def _split3(x):
    hi = x.astype(BF16)
    r = x - hi.astype(F32)
    lo = r.astype(BF16)
    lo2 = (r - lo.astype(F32)).astype(BF16)
    return hi, lo, lo2


def _sel_dot(sel, x):
    hi, lo, lo2 = _split3(x)
    return _dot(sel, hi) + _dot(sel, lo) + _dot(sel, lo2)


def _dot_sel(x, sel):
    hi, lo, lo2 = _split3(x)
    return _dot(hi, sel) + _dot(lo, sel) + _dot(lo2, sel)


def _softplus(x):
    return jnp.maximum(x, 0.0) + jnp.log(1.0 + jnp.exp(-jnp.abs(x)))


def _log_sigmoid(x):
    return jnp.minimum(x, 0.0) - jnp.log(1.0 + jnp.exp(-jnp.abs(x)))


def _sigmoid(x):
    return 1.0 / (1.0 + jnp.exp(-x))


def _silu(x):
    return x / (1.0 + jnp.exp(-x))


def _lane_iota(shape):
    return lax.broadcasted_iota(jnp.int32, shape, len(shape) - 1)


def _half_masks():
    lane = _lane_iota((1, LANES))
    return lane < HEAD_DIM, lane >= HEAD_DIM


def _params(semantics):
    return pltpu.CompilerParams(dimension_semantics=semantics, vmem_limit_bytes=VMEM_LIMIT)


def _proj_kernel(x_ref, g_ref, w_ref, o_ref, h_ref):
    @pl.when(pl.program_id(1) == 0)
    def _():
        x = x_ref[...]
        ms = jnp.mean(x * x, axis=-1, keepdims=True)
        h_ref[...] = (x * lax.rsqrt(ms + NORM_EPS) * g_ref[...]).astype(BF16)

    o_ref[...] = _dot(h_ref[...], w_ref[...]).astype(o_ref.dtype)


def _norm_proj(x2, g, w, out_dtype, tm, tn):
    t, d = x2.shape
    n = w.shape[1]
    return pl.pallas_call(
        _proj_kernel,
        grid=(t // tm, n // tn),
        in_specs=[
            pl.BlockSpec((tm, d), lambda i, j: (i, 0)),
            pl.BlockSpec((1, d), lambda i, j: (0, 0)),
            pl.BlockSpec((d, tn), lambda i, j: (0, j)),
        ],
        out_specs=pl.BlockSpec((tm, tn), lambda i, j: (i, j)),
        out_shape=jax.ShapeDtypeStruct((t, n), out_dtype),
        scratch_shapes=[pltpu.VMEM((tm, d), BF16)],
        compiler_params=_params(("arbitrary", "arbitrary")),
        name="norm_proj",
    )(x2, g, w)


def _fox_cum_kernel(f_ref, bias_ref, tri_ref, c_ref, carry_ref):
    @pl.when(pl.program_id(1) == 0)
    def _():
        carry_ref[...] = jnp.zeros_like(carry_ref)

    lf = _log_sigmoid(f_ref[...] + bias_ref[...])
    cs = _sel_dot(tri_ref[...], lf) + carry_ref[...]
    c_ref[...] = cs
    carry_ref[...] = cs[-1:, :]


def _fox_cum(small, bias_row, batch, seq, tr=256):
    tri = jnp.asarray(np.tril(np.ones((tr, tr), np.float32)), BF16)
    nb = seq // tr
    return pl.pallas_call(
        _fox_cum_kernel,
        grid=(batch, nb),
        in_specs=[
            pl.BlockSpec((tr, LANES), lambda b, r: (b * nb + r, 0)),
            pl.BlockSpec((1, LANES), lambda b, r: (0, 0)),
            pl.BlockSpec((tr, tr), lambda b, r: (0, 0)),
        ],
        out_specs=pl.BlockSpec((tr, LANES), lambda b, r: (b * nb + r, 0)),
        out_shape=jax.ShapeDtypeStruct(small.shape, F32),
        scratch_shapes=[pltpu.VMEM((1, LANES), F32)],
        compiler_params=_params(("arbitrary", "arbitrary")),
        name="fox_cum",
    )(small, bias_row, tri)


def _pair_rms(x, gain, half0, half1):
    sq = x * x
    ms0 = jnp.sum(jnp.where(half0, sq, 0.0), axis=-1, keepdims=True) * (1.0 / HEAD_DIM)
    ms1 = jnp.sum(jnp.where(half1, sq, 0.0), axis=-1, keepdims=True) * (1.0 / HEAD_DIM)
    r = jnp.where(half0, lax.rsqrt(ms0 + NORM_EPS), lax.rsqrt(ms1 + NORM_EPS))
    return x * r * gain


def _lane_pick(x, idx):
    lane = _lane_iota((1, LANES))
    return jnp.sum(jnp.where(lane == idx, x, 0.0), axis=-1, keepdims=True)


def _aug_lanes(c_col, base, key_side):
    lane = _lane_iota((1, LANES))
    hi, lo, lo2 = [v.astype(F32) for v in _split3(c_col)]
    if key_side:
        hi, lo, lo2 = -hi, -lo, -lo2
        vals, ones = (base + 3, base + 4, base + 5), (base, base + 1, base + 2)
    else:
        vals, ones = (base, base + 1, base + 2), (base + 3, base + 4, base + 5)
    out = jnp.where((lane == ones[0]) | (lane == ones[1]) | (lane == ones[2]), 1.0, 0.0)
    out = jnp.where(lane == vals[0], hi, out)
    out = jnp.where(lane == vals[1], lo, out)
    out = jnp.where(lane == vals[2], lo2, out)
    return out


def _fox_kernel(tab_ref, q_ref, k_ref, v_ref, cq_ref, ck_ref, gq_ref, gk_ref, o_ref,
                kaug_ref, acc_ref, m_ref, l_ref, *, seq, tq, rows_per_pass):
    b, p, i = pl.program_id(0), pl.program_id(1), pl.program_id(2)
    nq = seq // tq
    half = _half_masks()
    scale = HEAD_DIM ** -0.5

    @pl.when(i == 0)
    def _():
        def fill(r, carry):
            rows = pl.ds(pl.multiple_of(r * rows_per_pass, rows_per_pass), rows_per_pass)
            kn = _pair_rms(k_ref[rows, :].astype(F32), gk_ref[...], *half)
            cc = ck_ref[rows, :]
            for h in range(2):
                aug = _aug_lanes(_lane_pick(cc, 2 * p + h), HEAD_DIM * (1 - h), True)
                kaug_ref[h, rows, :] = jnp.where(half[h], kn, aug).astype(BF16)
            return carry
        lax.fori_loop(0, seq // rows_per_pass, fill, 0)

    qn = _pair_rms(q_ref[...].astype(F32), gq_ref[...], *half) * scale
    cq = cq_ref[...]
    row = lax.broadcasted_iota(jnp.int32, (tq, tq), 0)
    col = lax.broadcasted_iota(jnp.int32, (tq, tq), 1)
    thr = tab_ref[2 * N_HEADS * nq * pl.num_programs(0)]

    for h in range(2):
        head = 2 * p + h
        aug = _aug_lanes(_lane_pick(cq, head), HEAD_DIM * (1 - h), False)
        qa = jnp.where(half[h], qn, aug).astype(BF16)
        m_ref[...] = jnp.full_like(m_ref, NEG_BIG)
        l_ref[...] = jnp.zeros_like(l_ref)
        acc_ref[h] = jnp.zeros((tq, LANES), F32)

        def step(j, masked, h=h, qa=qa):
            rows = pl.ds(pl.multiple_of(j * tq, tq), tq)
            s = _dot_nt(qa, kaug_ref[h, rows, :])
            if masked:
                s = jnp.where(col <= row, s, NEG_BIG)
            m_prev = m_ref[...]
            m_new = jnp.maximum(m_prev, jnp.max(s, axis=1, keepdims=True))
            alpha = jnp.exp(m_prev - m_new)
            pr = jnp.exp(s - m_new)
            l_ref[...] = alpha * l_ref[...] + jnp.sum(pr, axis=1, keepdims=True)
            acc_ref[h] = alpha * acc_ref[h] + _dot(pr.astype(BF16), v_ref[rows, :])
            m_ref[...] = m_new

        step(i, True)
        tbase = (b * N_HEADS + head) * nq
        c_first = tab_ref[2 * (tbase + i)]

        def cond(j, tbase=tbase, c_first=c_first):
            c_last = tab_ref[2 * (tbase + jnp.maximum(j, 0)) + 1]
            return jnp.logical_and(j >= 0, c_first - c_last >= -thr)

        def body(j, step=step):
            step(j, False)
            return j - 1

        lax.while_loop(cond, body, i - 1)
        acc_ref[h] = acc_ref[h] / l_ref[...]

    o_ref[...] = jnp.where(half[0], acc_ref[0], acc_ref[1]).astype(o_ref.dtype)


def _fox_attention(big, c, tab, gq2, gk2, batch, seq, tq=256):
    nq = seq // tq
    t = batch * seq
    kern = functools.partial(_fox_kernel, seq=seq, tq=tq, rows_per_pass=min(512, seq))
    grid_spec = pltpu.PrefetchScalarGridSpec(
        num_scalar_prefetch=1,
        grid=(batch, N_HEADS // 2, nq),
        in_specs=[
            pl.BlockSpec((tq, LANES), lambda b, p, i, tab: (b * nq + i, C_FOX + p)),
            pl.BlockSpec((seq, LANES), lambda b, p, i, tab: (b, C_FOX + 4 + p)),
            pl.BlockSpec((seq, LANES), lambda b, p, i, tab: (b, C_FOX + 8 + p)),
            pl.BlockSpec((tq, LANES), lambda b, p, i, tab: (b * nq + i, 0)),
            pl.BlockSpec((seq, LANES), lambda b, p, i, tab: (b, 0)),
            pl.BlockSpec((1, LANES), lambda b, p, i, tab: (0, 0)),
            pl.BlockSpec((1, LANES), lambda b, p, i, tab: (0, 0)),
        ],
        out_specs=pl.BlockSpec((tq, LANES), lambda b, p, i, tab: (b * nq + i, p)),
        scratch_shapes=[
            pltpu.VMEM((2, seq, LANES), BF16),
            pltpu.VMEM((2, tq, LANES), F32),
            pltpu.VMEM((tq, 1), F32),
            pltpu.VMEM((tq, 1), F32),
        ],
    )
    return pl.pallas_call(
        kern,
        grid_spec=grid_spec,
        out_shape=jax.ShapeDtypeStruct((t, BRANCH_WIDTH), BF16),
        compiler_params=_params(("arbitrary", "arbitrary", "arbitrary")),
        name="fox_attention",
    )(tab, big, big, big, c, c, gq2, gk2)


def _sb_kernel(q_ref, k_ref, v_ref, gq_ref, gk_ref, upper_ref, o_ref,
               kn_ref, acc_ref, run_ref, flag_ref, *, seq, tq, rows_per_pass):
    i = pl.program_id(2)
    half = _half_masks()
    scale = HEAD_DIM ** -0.5

    @pl.when(i == 0)
    def _():
        def fill(r, carry):
            rows = pl.ds(pl.multiple_of(r * rows_per_pass, rows_per_pass), rows_per_pass)
            kn_ref[rows, :] = _pair_rms(k_ref[rows, :].astype(F32), gk_ref[...], *half).astype(BF16)
            return carry
        lax.fori_loop(0, seq // rows_per_pass, fill, 0)

    qn = _pair_rms(q_ref[...].astype(F32), gq_ref[...], *half) * scale
    row = lax.broadcasted_iota(jnp.int32, (tq, tq), 0)
    col = lax.broadcasted_iota(jnp.int32, (tq, tq), 1)

    for h in range(2):
        qh = jnp.where(half[h], qn, 0.0).astype(BF16)
        run_ref[...] = jnp.zeros_like(run_ref)
        acc_ref[h] = jnp.zeros((tq, LANES), F32)

        def step(j, masked, h=h, qh=qh):
            rows = pl.ds(pl.multiple_of(j * tq, tq), tq)
            z = _dot_nt(qh, kn_ref[rows, :])
            log_beta = _log_sigmoid(z)
            log_rest = log_beta - z
            if masked:
                log_rest = jnp.where(col < row, log_rest, 0.0)
            hi, lo = _split2(log_rest)
            inner = _dot(hi, upper_ref[...]) + _dot(lo, upper_ref[...])
            run = run_ref[...]
            w = jnp.exp(log_beta + inner + run)
            if masked:
                w = jnp.where(col < row, w, 0.0)
            acc_ref[h] = acc_ref[h] + _dot(w.astype(BF16), v_ref[rows, :])
            run = run + jnp.sum(log_rest, axis=1, keepdims=True)
            run_ref[...] = run
            flag_ref[0] = jnp.max(run)

        step(i, True)

        def cond(j):
            return jnp.logical_and(j >= 0, flag_ref[0] > -EXP_UNDERFLOW)

        def body(j, step=step):
            step(j, False)
            return j - 1

        lax.while_loop(cond, body, i - 1)

    o_ref[...] = jnp.where(half[0], acc_ref[0], acc_ref[1]).astype(o_ref.dtype)


def _sb_attention(big, gq2, gk2, batch, seq, tq=256):
    nq = seq // tq
    t = batch * seq
    upper = jnp.asarray(np.tril(np.ones((tq, tq), np.float32), -1), BF16)
    kern = functools.partial(_sb_kernel, seq=seq, tq=tq, rows_per_pass=min(512, seq))
    return pl.pallas_call(
        kern,
        grid=(batch, N_HEADS // 2, nq),
        in_specs=[
            pl.BlockSpec((tq, LANES), lambda b, p, i: (b * nq + i, C_SB + p)),
            pl.BlockSpec((seq, LANES), lambda b, p, i: (b, C_SB + 4 + p)),
            pl.BlockSpec((seq, LANES), lambda b, p, i: (b, C_SB + 8 + p)),
            pl.BlockSpec((1, LANES), lambda b, p, i: (0, 0)),
            pl.BlockSpec((1, LANES), lambda b, p, i: (0, 0)),
            pl.BlockSpec((tq, tq), lambda b, p, i: (0, 0)),
        ],
        out_specs=pl.BlockSpec((tq, LANES), lambda b, p, i: (b * nq + i, p)),
        out_shape=jax.ShapeDtypeStruct((t, BRANCH_WIDTH), BF16),
        scratch_shapes=[
            pltpu.VMEM((seq, LANES), BF16),
            pltpu.VMEM((2, tq, LANES), F32),
            pltpu.VMEM((tq, 1), F32),
            pltpu.SMEM((1,), F32),
        ],
        compiler_params=_params(("arbitrary", "arbitrary", "arbitrary")),
        name="sb_attention",
    )(big, big, big, gq2, gk2, upper)


def _gdn_prep_kernel(xq_ref, xk_ref, xv_ref, hq_ref, hk_ref, hv_ref, small_ref, cw_ref, alog_ref, dtb_ref,
                     grp_ref, eg_ref, eb_ref, tri_ref, q_ref, k_ref, v_ref, gc_ref, beta_ref, *, tiles_per_seq):
    first = (pl.program_id(0) % tiles_per_seq) == 0

    def conv_silu(x_ref, halo_ref, sec):
        halo = jnp.where(first, 0.0, halo_ref[...].astype(F32))
        xx = jnp.concatenate([halo, x_ref[...].astype(F32)], axis=0)
        hr = halo.shape[0]
        cols = slice(sec * BRANCH_WIDTH, (sec + 1) * BRANCH_WIDTH)
        y = cw_ref[GDN_CONV - 1:GDN_CONV, cols] * xx[hr:, :]
        for d in range(1, GDN_CONV):
            y = y + cw_ref[GDN_CONV - 1 - d:GDN_CONV - d, cols] * pltpu.roll(xx, d, 0)[hr:, :]
        return _silu(y)

    def l2(u):
        ss = _dot_sel(u * u, grp_ref[...])
        return u * lax.rsqrt(ss + NORM_EPS)

    q_ref[...] = l2(conv_silu(xq_ref, hq_ref, 0)) * (HEAD_DIM ** -0.5)
    k_ref[...] = l2(conv_silu(xk_ref, hk_ref, 1))
    v_ref[...] = conv_silu(xv_ref, hv_ref, 2)

    sm = small_ref[...]
    g = -jnp.exp(alog_ref[...]) * _softplus(sm + dtb_ref[...])
    g_wide = _dot_sel(g, eg_ref[...])
    gc_ref[...] = _sel_dot(tri_ref[...], g_wide)
    beta_ref[...] = _dot_sel(_sigmoid(sm), eb_ref[...])


def _gdn_prep(big, small, conv_w, alog_row, dtb_row, batch, seq, tm=256):
    t = batch * seq
    halo_rows = 16
    grp = np.kron(np.eye(N_HEADS, dtype=np.float32), np.ones((HEAD_DIM, HEAD_DIM), np.float32))
    eg = np.zeros((LANES, BRANCH_WIDTH), np.float32)
    eb = np.zeros((LANES, BRANCH_WIDTH), np.float32)
    for h in range(N_HEADS):
        eg[L_GDNA + h, h * HEAD_DIM:(h + 1) * HEAD_DIM] = 1.0
        eb[L_GDNB + h, h * HEAD_DIM:(h + 1) * HEAD_DIM] = 1.0
    tri = np.kron(np.eye(tm // CHUNK, dtype=np.float32), np.tril(np.ones((CHUNK, CHUNK), np.float32)))
    wide = jax.ShapeDtypeStruct((t, BRANCH_WIDTH), F32)
    ratio = tm // halo_rows
    kern = functools.partial(_gdn_prep_kernel, tiles_per_seq=seq // tm)
    const = lambda shape: pl.BlockSpec(shape, lambda i: (0, 0))
    out_spec = pl.BlockSpec((tm, BRANCH_WIDTH), lambda i: (i, 0))
    sec0 = C_GDN * LANES // BRANCH_WIDTH
    x_specs = [pl.BlockSpec((tm, BRANCH_WIDTH), functools.partial(lambda i, s: (i, sec0 + s), s=s))
               for s in range(3)]
    halo_specs = [pl.BlockSpec((halo_rows, BRANCH_WIDTH),
                               functools.partial(lambda i, s: (jnp.maximum(i * ratio - 1, 0), sec0 + s), s=s))
                  for s in range(3)]
    return pl.pallas_call(
        kern,
        grid=(t // tm,),
        in_specs=x_specs + halo_specs + [
            pl.BlockSpec((tm, LANES), lambda i: (i, 0)),
            const((GDN_CONV, 3 * BRANCH_WIDTH)),
            const((1, LANES)),
            const((1, LANES)),
            const((BRANCH_WIDTH, BRANCH_WIDTH)),
            const((LANES, BRANCH_WIDTH)),
            const((LANES, BRANCH_WIDTH)),
            const((tm, tm)),
        ],
        out_specs=[out_spec] * 5,
        out_shape=[wide] * 5,
        compiler_params=_params(("arbitrary",)),
        name="gdn_prep",
    )(big, big, big, big, big, big, small, conv_w, alog_row, dtb_row, jnp.asarray(grp, BF16),
      jnp.asarray(eg, BF16), jnp.asarray(eb, BF16), jnp.asarray(tri, BF16))


def _stack_heads(x, half):
    return jnp.concatenate([jnp.where(half[0], x, 0.0), jnp.where(half[1], x, 0.0)], axis=0)


def _unstack(x):
    return x[:CHUNK, :] + x[CHUNK:, :]


def _gdn_kernel(q_ref, k_ref, v_ref, gc_ref, beta_ref, gain_ref, avg_ref, o_ref, state_ref, *, chunks):
    @pl.when(pl.program_id(1) == 0)
    def _():
        state_ref[...] = jnp.zeros_like(state_ref)

    half = _half_masks()
    n2 = 2 * CHUNK
    r = lax.broadcasted_iota(jnp.int32, (n2, n2), 0)
    c = lax.broadcasted_iota(jnp.int32, (n2, n2), 1)
    same = (r < CHUNK) == (c < CHUNK)
    lower = same & (c <= r)
    strict = same & (c < r)
    eye = (r == c).astype(F32)
    lane = _lane_iota((1, LANES))
    sub = lane % HEAD_DIM

    for ci in range(chunks):
        rows = slice(ci * CHUNK, (ci + 1) * CHUNK)
        for p in range(N_HEADS // 2):
            cols = slice(p * LANES, (p + 1) * LANES)
            q2, k2, v2 = q_ref[rows, cols], k_ref[rows, cols], v_ref[rows, cols]
            gc2, beta2 = gc_ref[rows, cols], beta_ref[rows, cols]

            hi, lo, lo2 = [u.astype(F32) for u in _split3(gc2)]
            one = jnp.where((sub >= 3) & (sub < 6), 1.0, 0.0)
            la = jnp.where(sub == 0, hi, jnp.where(sub == 1, lo, jnp.where(sub == 2, lo2, one)))
            one = jnp.where(sub < 3, 1.0, 0.0)
            ra = jnp.where(sub == 3, -hi, jnp.where(sub == 4, -lo, jnp.where(sub == 5, -lo2, one)))
            ra = ra.astype(BF16)
            diff = _dot_nt(_stack_heads(la, half).astype(BF16), jnp.concatenate([ra, ra], axis=0))
            decay = jnp.where(lower, jnp.exp(jnp.minimum(diff, 0.0)), 0.0)

            eg = jnp.exp(gc2)
            kb = k2 * beta2
            kk = k2.astype(BF16)
            kk2 = jnp.concatenate([kk, kk], axis=0)
            a = jnp.where(strict, _dot_nt(_stack_heads(kb, half).astype(BF16), kk2) * decay, 0.0)
            intra = _dot_nt(_stack_heads(q2, half).astype(BF16), kk2) * decay

            x = eye - a
            pw = a
            for _ in range(5):
                pb = pw.astype(BF16)
                pw = _dot(pb, pb)
                x = x + _dot(x.astype(BF16), pw.astype(BF16))

            rhs = jnp.concatenate([_stack_heads(v2 * beta2, half), _stack_heads(kb * eg, half)], axis=1)
            uw = _dot(x.astype(BF16), rhs.astype(BF16))
            u2 = _unstack(uw[:, :LANES])
            w2 = _unstack(uw[:, LANES:])

            state = state_ref[p]
            sb16 = state.astype(BF16)
            v_new = u2 - _dot(w2.astype(BF16), sb16)
            o2 = _dot((q2 * eg).astype(BF16), sb16) + _unstack(
                _dot(intra.astype(BF16), _stack_heads(v_new, half).astype(BF16)))
            k_dec = k2 * jnp.exp(gc2[CHUNK - 1:CHUNK, :] - gc2)
            upd = _dot_tn(k_dec.astype(BF16), v_new.astype(BF16))
            state_ref[p] = state * eg[CHUNK - 1:CHUNK, :] + jnp.where(same, upd, 0.0)

            ms = _dot_sel(o2 * o2, avg_ref[...])
            o_ref[rows, cols] = (o2 * lax.rsqrt(ms + NORM_EPS) * gain_ref[...]).astype(o_ref.dtype)


def _gdn_scan(qn, kn, vv, gc, beta, gain2, batch, seq, chunks=2):
    t = batch * seq
    tm = chunks * CHUNK
    nb = seq // tm
    avg = np.kron(np.eye(2, dtype=np.float32), np.full((HEAD_DIM, HEAD_DIM), 1.0 / HEAD_DIM, np.float32))
    spec = pl.BlockSpec((tm, BRANCH_WIDTH), lambda b, n: (b * nb + n, 0))
    return pl.pallas_call(
        functools.partial(_gdn_kernel, chunks=chunks),
        grid=(batch, nb),
        in_specs=[spec] * 5 + [pl.BlockSpec((1, LANES), lambda b, n: (0, 0)),
                               pl.BlockSpec((LANES, LANES), lambda b, n: (0, 0))],
        out_specs=spec,
        out_shape=jax.ShapeDtypeStruct((t, BRANCH_WIDTH), BF16),
        scratch_shapes=[pltpu.VMEM((N_HEADS // 2, LANES, LANES), F32)],
        compiler_params=_params(("arbitrary", "arbitrary")),
        name="gdn_scan",
    )(qn, kn, vv, gc, beta, gain2, jnp.asarray(avg, BF16))


GLA_LEVELS = 6


def _gla_constants():
    n = CHUNK
    t = np.arange(n)
    blocks = []
    masks = []
    for lev in range(GLA_LEVELS):
        w = 1 << lev
        lower = (t // w) % 2 == 1
        mid_q = (t // w) * w
        cq = (lower[:, None] & (t[None, :] > mid_q[:, None]) & (t[None, :] <= t[:, None])).astype(np.float32)
        mid_k = (t // w + 1) * w
        ck = ((~lower)[:, None] & (t[None, :] > t[:, None]) & (t[None, :] <= mid_k[:, None])).astype(np.float32)
        blocks += [cq, ck]
        m = lower[:, None] & (~lower)[None, :] & ((t // (2 * w))[:, None] == (t // (2 * w))[None, :])
        masks.append(np.kron(np.eye(2), m.astype(np.float32)))
    masks.append(np.eye(2 * n, dtype=np.float32))
    tri = np.tril(np.ones((n, n), np.float32))
    rev = (t[None, :] > t[:, None]).astype(np.float32)
    blocks += [tri, rev]
    return np.concatenate(blocks, axis=0), np.stack(masks)


def _gla_kernel(x_ref, small_ref, w2_ref, bias_ref, sel_ref, mask_ref, gain_ref, o_ref, state_ref, *, chunks):
    @pl.when(pl.program_id(1) == 0)
    def _():
        state_ref[...] = jnp.zeros_like(state_ref)

    half = _half_masks()
    qk_w = GLA_HEADS * HEAD_DIM
    r = lax.broadcasted_iota(jnp.int32, (2 * LANES, LANES), 0)
    c = lax.broadcasted_iota(jnp.int32, (2 * LANES, LANES), 1)
    state_mask = (r < LANES) == (c < HEAD_DIM)
    zeros_v = jnp.zeros((CHUNK, LANES), F32)

    for ci in range(chunks):
        rows = slice(ci * CHUNK, (ci + 1) * CHUNK)
        y = _dot(small_ref[rows, :].astype(BF16), w2_ref[...]) + bias_ref[...]
        log_a = _log_sigmoid(y) * (1.0 / GLA_GATE_TAU)
        for p in range(GLA_HEADS // 2):
            cols = slice(p * LANES, (p + 1) * LANES)
            q2 = x_ref[rows, p * LANES:(p + 1) * LANES].astype(F32) * (HEAD_DIM ** -0.5)
            k2 = x_ref[rows, qk_w + p * LANES:qk_w + (p + 1) * LANES].astype(F32)
            v_pair = x_ref[rows, 2 * qk_w + 2 * p * LANES:2 * qk_w + 2 * (p + 1) * LANES]
            la = log_a[:, cols]
            hi, lo = _split2(la)
            sums = _dot(sel_ref[...], jnp.concatenate([hi, lo], axis=1))
            sums = sums[:, :LANES] + sums[:, LANES:]

            a = None
            for lev in range(GLA_LEVELS + 1):
                if lev < GLA_LEVELS:
                    dq = sums[(2 * lev) * CHUNK:(2 * lev + 1) * CHUNK, :]
                    dk = sums[(2 * lev + 1) * CHUNK:(2 * lev + 2) * CHUNK, :]
                    ql, kl = q2 * jnp.exp(dq), k2 * jnp.exp(dk)
                else:
                    ql, kl = q2, k2
                kb = kl.astype(BF16)
                prod = _dot_nt(_stack_heads(ql, half).astype(BF16), jnp.concatenate([kb, kb], axis=0))
                term = prod * mask_ref[lev]
                a = term if a is None else a + term

            b = sums[2 * GLA_LEVELS * CHUNK:(2 * GLA_LEVELS + 1) * CHUNK, :]
            b_rev = sums[(2 * GLA_LEVELS + 1) * CHUNK:, :]
            vf = v_pair.astype(F32)
            v_stack = jnp.concatenate(
                [jnp.concatenate([vf[:, :LANES], zeros_v], axis=1),
                 jnp.concatenate([zeros_v, vf[:, LANES:]], axis=1)], axis=0).astype(BF16)
            state = state_ref[p]
            o_pair = _unstack(_dot(a.astype(BF16), v_stack)) + _dot_nt(
                (q2 * jnp.exp(b)).astype(BF16), state.astype(BF16))
            k_dec = (k2 * jnp.exp(b_rev)).astype(BF16)
            upd = _dot_tn(v_pair, k_dec)
            state_ref[p] = state * jnp.exp(b[CHUNK - 1:CHUNK, :]) + jnp.where(state_mask, upd, 0.0)

            for h in range(2):
                oh = o_pair[:, h * LANES:(h + 1) * LANES]
                ms = jnp.mean(oh * oh, axis=-1, keepdims=True)
                col0 = (2 * p + h) * LANES
                o_ref[rows, col0:col0 + LANES] = (oh * lax.rsqrt(ms + NORM_EPS) * gain_ref[...]).astype(o_ref.dtype)


def _gla(big, small, w2_ext, bias_row, gain_row, batch, seq, chunks=2):
    t = batch * seq
    tm = chunks * CHUNK
    nb = seq // tm
    sel, masks = _gla_constants()
    width = 2 * GLA_HEADS * HEAD_DIM + BRANCH_WIDTH
    const2 = lambda shape: pl.BlockSpec(shape, lambda b, n: (0, 0))
    return pl.pallas_call(
        functools.partial(_gla_kernel, chunks=chunks),
        grid=(batch, nb),
        in_specs=[
            pl.BlockSpec((tm, width), lambda b, n: (b * nb + n, C_GLA * LANES // width)),
            pl.BlockSpec((tm, LANES), lambda b, n: (b * nb + n, 0)),
            const2((LANES, GLA_HEADS * HEAD_DIM)),
            const2((1, GLA_HEADS * HEAD_DIM)),
            const2(sel.shape),
            pl.BlockSpec(masks.shape, lambda b, n: (0, 0, 0)),
            const2((1, LANES)),
        ],
        out_specs=pl.BlockSpec((tm, BRANCH_WIDTH), lambda b, n: (b * nb + n, 0)),
        out_shape=jax.ShapeDtypeStruct((t, BRANCH_WIDTH), BF16),
        scratch_shapes=[pltpu.VMEM((GLA_HEADS // 2, 2 * LANES, LANES), F32)],
        compiler_params=_params(("arbitrary", "arbitrary")),
        name="gla",
    )(big, small, w2_ext, bias_row, jnp.asarray(sel, BF16), jnp.asarray(masks, F32), gain_row)


def _merge_kernel(x_ref, o0_ref, o1_ref, o2_ref, o3_ref, z_ref, ml_ref, gb_ref, wb_ref, wo_ref, out_ref):
    merged = None
    for n, o_ref in enumerate((o0_ref, o1_ref, o2_ref, o3_ref)):
        z = z_ref[:, n * BRANCH_WIDTH:(n + 1) * BRANCH_WIDTH].astype(F32)
        br = (o_ref[...].astype(F32) * _silu(z)).astype(BF16)
        proj = _dot(br, wb_ref[n])
        gate = _sigmoid(ml_ref[:, n * D_MODEL:(n + 1) * D_MODEL].astype(F32)
                        + gb_ref[:, n * D_MODEL:(n + 1) * D_MODEL])
        term = proj * gate
        merged = term if merged is None else merged + term
    out_ref[...] = x_ref[...] + _dot(merged.astype(BF16), wo_ref[...])


def _merge(x2, outs, big, gate_bias_row, w_branch, w_out, tm=256):
    t = x2.shape[0]
    zw = N_BRANCH * BRANCH_WIDTH
    mw = N_BRANCH * D_MODEL
    o_spec = pl.BlockSpec((tm, BRANCH_WIDTH), lambda i: (i, 0))
    return pl.pallas_call(
        _merge_kernel,
        grid=(t // tm,),
        in_specs=[
            pl.BlockSpec((tm, D_MODEL), lambda i: (i, 0)),
            o_spec, o_spec, o_spec, o_spec,
            pl.BlockSpec((tm, zw), lambda i: (i, C_Z * LANES // zw)),
            pl.BlockSpec((tm, mw), lambda i: (i, C_MERGE * LANES // mw)),
            pl.BlockSpec((1, mw), lambda i: (0, 0)),
            pl.BlockSpec((N_BRANCH, BRANCH_WIDTH, D_MODEL), lambda i: (0, 0, 0)),
            pl.BlockSpec((D_MODEL, D_MODEL), lambda i: (0, 0)),
        ],
        out_specs=pl.BlockSpec((tm, D_MODEL), lambda i: (i, 0)),
        out_shape=jax.ShapeDtypeStruct((t, D_MODEL), F32),
        compiler_params=_params(("arbitrary",)),
        name="merge_out",
    )(x2, *outs, big, big, gate_bias_row, w_branch, w_out)


def _reorder_w_in(w_in):
    sizes = (3 * BRANCH_WIDTH, N_HEADS, BRANCH_WIDTH, 3 * BRANCH_WIDTH, BRANCH_WIDTH,
             3 * BRANCH_WIDTH, N_HEADS, N_HEADS, BRANCH_WIDTH,
             2 * GLA_HEADS * HEAD_DIM, BRANCH_WIDTH, GLA_GATE_RANK, BRANCH_WIDTH, N_BRANCH * D_MODEL)
    (fox_qkv, fox_f, fox_z, sb_qkv, sb_z, gdn_qkv, gdn_a, gdn_b, gdn_z,
     gla_qk, gla_v, gla_g, gla_z, merge) = jnp.split(w_in, np.cumsum(sizes)[:-1].tolist(), axis=1)
    big = jnp.concatenate([merge, fox_z, sb_z, gdn_z, gla_z, gla_qk, gla_v, gdn_qkv, fox_qkv, sb_qkv], axis=1)
    pad = jnp.zeros((w_in.shape[0], LANES - (3 * N_HEADS + GLA_GATE_RANK)), w_in.dtype)
    small = jnp.concatenate([fox_f, gdn_a, gdn_b, gla_g, pad], axis=1)
    return big.astype(BF16), small.astype(BF16)


def _lane_row(values, offset):
    row = jnp.zeros((1, LANES), F32)
    return row.at[0, offset:offset + values.shape[0]].set(values.astype(F32))


def _layer(x2, batch, seq, norm_g, w_in, fox_f_bias, fox_q_norm, fox_k_norm, sb_q_norm, sb_k_norm,
           gdn_conv_w, gdn_a_log, gdn_dt_bias, gdn_o_norm, gla_gate_w2, gla_gate_bias, gla_o_norm,
           gate_bias, w_branch, w_out):
    w_big, w_small = _reorder_w_in(w_in)
    g_row = norm_g.reshape(1, D_MODEL).astype(F32)
    t = batch * seq
    tm = min(1024, t)
    big = _norm_proj(x2, g_row, w_big, BF16, tm, 512)
    small = _norm_proj(x2, g_row, w_small, F32, tm, LANES)

    pair = lambda g: jnp.tile(g.astype(F32), 2).reshape(1, LANES)

    c = _fox_cum(small, _lane_row(fox_f_bias, L_FOXF), batch, seq)
    tq = min(256, seq)
    nq = seq // tq
    c3 = c.reshape(batch, seq, LANES)[:, :, :N_HEADS]
    first = jnp.transpose(c3[:, 0::tq, :], (0, 2, 1))
    last = jnp.transpose(c3[:, tq - 1::tq, :], (0, 2, 1))
    qk_bound = HEAD_DIM ** 0.5 * jnp.max(jnp.abs(fox_q_norm)) * jnp.max(jnp.abs(fox_k_norm))
    thr = EXP_UNDERFLOW + 2.0 * 1.02 * qk_bound + 1.0
    tab = jnp.concatenate([jnp.stack([first, last], axis=-1).reshape(-1), thr.reshape(1)]).astype(F32)
    o_fox = _fox_attention(big, c, tab, pair(fox_q_norm), pair(fox_k_norm), batch, seq, tq)

    o_sb = _sb_attention(big, pair(sb_q_norm), pair(sb_k_norm), batch, seq, tq)

    qn, kn, vv, gc, beta = _gdn_prep(big, small, gdn_conv_w.astype(F32), _lane_row(gdn_a_log, L_GDNA),
                                     _lane_row(gdn_dt_bias, L_GDNA), batch, seq, min(256, seq))
    o_gdn = _gdn_scan(qn, kn, vv, gc, beta, pair(gdn_o_norm), batch, seq)

    w2_ext = jnp.zeros((LANES, GLA_HEADS * HEAD_DIM), F32).at[L_GLAG:L_GLAG + GLA_GATE_RANK].set(
        gla_gate_w2.astype(F32)).astype(BF16)
    o_gla = _gla(big, small, w2_ext, gla_gate_bias.reshape(1, -1).astype(F32),
                 gla_o_norm.reshape(1, LANES).astype(F32), batch, seq)

    return _merge(x2, (o_fox, o_sb, o_gdn, o_gla), big, gate_bias.reshape(1, -1).astype(F32),
                  w_branch.astype(BF16), w_out.astype(BF16), min(256, t))


def kernel(x, norm_g, w_in, fox_f_bias, fox_q_norm, fox_k_norm, sb_q_norm, sb_k_norm, gdn_conv_w,
           gdn_a_log, gdn_dt_bias, gdn_o_norm, gla_gate_w2, gla_gate_bias, gla_o_norm, gate_bias,
           w_branch, w_out):
    batch, seq, d = x.shape
    x2 = x.reshape(batch * seq, d)
    params = (norm_g, w_in, fox_f_bias, fox_q_norm, fox_k_norm, sb_q_norm, sb_k_norm, gdn_conv_w,
              gdn_a_log, gdn_dt_bias, gdn_o_norm, gla_gate_w2, gla_gate_bias, gla_o_norm, gate_bias,
              w_branch, w_out)
    for l in range(norm_g.shape[0]):
        x2 = _layer(x2, batch, seq, *[p[l] for p in params])
    return x2.reshape(batch, seq, d)
```

```python
import functools

import numpy as np
import jax
import jax.numpy as jnp
from jax import lax
from jax.experimental import pallas as pl
from jax.experimental.pallas import tpu as pltpu

F32 = jnp.float32
BF16 = jnp.bfloat16

LANES = 128
HEAD_DIM = 64
D_MODEL = 1024
BRANCH_WIDTH = 512
N_BRANCH = 4
N_HEADS = 8
GLA_HEADS = 4
GLA_GATE_RANK = 16
GLA_GATE_TAU = 16.0
GDN_CONV = 4
CHUNK = 64
NORM_EPS = 1e-6
VMEM_LIMIT = 48 * 1024 * 1024
EXP_UNDERFLOW = 104.0
NEG_BIG = -1e30

C_MERGE, C_Z, C_GLA, C_GDN, C_FOX, C_SB, C_END = 0, 32, 48, 56, 68, 80, 92
N_BIG = C_END * LANES
L_FOXF, L_GDNA, L_GDNB, L_GLAG = 0, 8, 16, 24


def _dot(a, b):
    return jnp.dot(a, b, preferred_element_type=F32)


def _dot_nt(a, b):
    return lax.dot_general(a, b, (((1,), (1,)), ((), ())), preferred_element_type=F32)


def _dot_tn(a, b):
    return lax.dot_general(a, b, (((0,), (0,)), ((), ())), preferred_element_type=F32)


def _split2(x):
    hi = x.astype(BF16)
    lo = (x - hi.astype(F32)).astype(BF16)
    return hi, lo


def _split3(x):
    hi = x.astype(BF16)
    r = x - hi.astype(F32)
    lo = r.astype(BF16)
    lo2 = (r - lo.astype(F32)).astype(BF16)
    return hi, lo, lo2


def _sel_dot(sel, x):
    hi, lo, lo2 = _split3(x)
    return _dot(sel, hi) + _dot(sel, lo) + _dot(sel, lo2)


def _dot_sel(x, sel):
    hi, lo, lo2 = _split3(x)
    return _dot(hi, sel) + _dot(lo, sel) + _dot(lo2, sel)


def _softplus(x):
    return jnp.maximum(x, 0.0) + jnp.log(1.0 + jnp.exp(-jnp.abs(x)))


def _log_sigmoid(x):
    return jnp.minimum(x, 0.0) - jnp.log(1.0 + jnp.exp(-jnp.abs(x)))


def _sigmoid(x):
    return 1.0 / (1.0 + jnp.exp(-x))


def _silu(x):
    return x / (1.0 + jnp.exp(-x))


def _lane_iota(shape):
    return lax.broadcasted_iota(jnp.int32, shape, len(shape) - 1)


def _half_masks():
    lane = _lane_iota((1, LANES))
    return lane < HEAD_DIM, lane >= HEAD_DIM


def _params(semantics):
    return pltpu.CompilerParams(dimension_semantics=semantics, vmem_limit_bytes=VMEM_LIMIT)


def _proj_kernel(x_ref, g_ref, w_ref, o_ref, h_ref):
    @pl.when(pl.program_id(1) == 0)
    def _():
        x = x_ref[...]
        ms = jnp.mean(x * x, axis=-1, keepdims=True)
        h_ref[...] = (x * lax.rsqrt(ms + NORM_EPS) * g_ref[...]).astype(BF16)

    o_ref[...] = _dot(h_ref[...], w_ref[...]).astype(o_ref.dtype)


def _norm_proj(x2, g, w, out_dtype, tm, tn):
    t, d = x2.shape
    n = w.shape[1]
    return pl.pallas_call(
        _proj_kernel,
        grid=(t // tm, n // tn),
        in_specs=[
            pl.BlockSpec((tm, d), lambda i, j: (i, 0)),
            pl.BlockSpec((1, d), lambda i, j: (0, 0)),
            pl.BlockSpec((d, tn), lambda i, j: (0, j)),
        ],
        out_specs=pl.BlockSpec((tm, tn), lambda i, j: (i, j)),
        out_shape=jax.ShapeDtypeStruct((t, n), out_dtype),
        scratch_shapes=[pltpu.VMEM((tm, d), BF16)],
        compiler_params=_params(("arbitrary", "arbitrary")),
        name="norm_proj",
    )(x2, g, w)


def _fox_cum_kernel(f_ref, bias_ref, tri_ref, c_ref, carry_ref):
    @pl.when(pl.program_id(1) == 0)
    def _():
        carry_ref[...] = jnp.zeros_like(carry_ref)

    lf = _log_sigmoid(f_ref[...] + bias_ref[...])
    cs = _sel_dot(tri_ref[...], lf) + carry_ref[...]
    c_ref[...] = cs
    carry_ref[...] = cs[-1:, :]


def _fox_cum(small, bias_row, batch, seq, tr=256):
    tri = jnp.asarray(np.tril(np.ones((tr, tr), np.float32)), BF16)
    nb = seq // tr
    return pl.pallas_call(
        _fox_cum_kernel,
        grid=(batch, nb),
        in_specs=[
            pl.BlockSpec((tr, LANES), lambda b, r: (b * nb + r, 0)),
            pl.BlockSpec((1, LANES), lambda b, r: (0, 0)),
            pl.BlockSpec((tr, tr), lambda b, r: (0, 0)),
        ],
        out_specs=pl.BlockSpec((tr, LANES), lambda b, r: (b * nb + r, 0)),
        out_shape=jax.ShapeDtypeStruct(small.shape, F32),
        scratch_shapes=[pltpu.VMEM((1, LANES), F32)],
        compiler_params=_params(("arbitrary", "arbitrary")),
        name="fox_cum",
    )(small, bias_row, tri)


def _pair_rms(x, gain, half0, half1):
    sq = x * x
    ms0 = jnp.sum(jnp.where(half0, sq, 0.0), axis=-1, keepdims=True) * (1.0 / HEAD_DIM)
    ms1 = jnp.sum(jnp.where(half1, sq, 0.0), axis=-1, keepdims=True) * (1.0 / HEAD_DIM)
    r = jnp.where(half0, lax.rsqrt(ms0 + NORM_EPS), lax.rsqrt(ms1 + NORM_EPS))
    return x * r * gain


def _lane_pick(x, idx):
    lane = _lane_iota((1, LANES))
    return jnp.sum(jnp.where(lane == idx, x, 0.0), axis=-1, keepdims=True)


def _aug_lanes(c_col, base, key_side):
    lane = _lane_iota((1, LANES))
    hi, lo, lo2 = [v.astype(F32) for v in _split3(c_col)]
    if key_side:
        hi, lo, lo2 = -hi, -lo, -lo2
        vals, ones = (base + 3, base + 4, base + 5), (base, base + 1, base + 2)
    else:
        vals, ones = (base, base + 1, base + 2), (base + 3, base + 4, base + 5)
    out = jnp.where((lane == ones[0]) | (lane == ones[1]) | (lane == ones[2]), 1.0, 0.0)
    out = jnp.where(lane == vals[0], hi, out)
    out = jnp.where(lane == vals[1], lo, out)
    out = jnp.where(lane == vals[2], lo2, out)
    return out


FOX_STATIC_MAX_LIMIT = 20.0


def _fox_kernel(tab_ref, q_ref, k_ref, v_ref, cq_ref, ck_ref, gq_ref, gk_ref, o_ref,
                kaug_ref, vaug_ref, acc_ref, m_ref, l_ref, *, seq, tq, tk, rows_per_pass):
    b, p, i = pl.program_id(0), pl.program_id(1), pl.program_id(2)
    nq, nk = seq // tq, seq // tk
    nb = pl.num_programs(0)
    half = _half_masks()
    lane = _lane_iota((1, LANES))
    scale = HEAD_DIM ** -0.5
    spare = [HEAD_DIM * (1 - h) for h in range(2)]

    @pl.when(i == 0)
    def _():
        def fill(r, carry):
            rows = pl.ds(pl.multiple_of(r * rows_per_pass, rows_per_pass), rows_per_pass)
            kn = _pair_rms(k_ref[rows, :].astype(F32), gk_ref[...], *half)
            cc = ck_ref[rows, :]
            vv = v_ref[rows, :]
            for h in range(2):
                aug = _aug_lanes(_lane_pick(cc, 2 * p + h), spare[h], True)
                kaug_ref[h, rows, :] = jnp.where(half[h], kn, aug).astype(BF16)
                ones = jnp.where(lane == spare[h], 1.0, 0.0).astype(BF16)
                vaug_ref[h, rows, :] = jnp.where(half[h], vv, ones)
            return carry
        lax.fori_loop(0, seq // rows_per_pass, fill, 0)

    first_off = N_HEADS * nb * nq
    thr = tab_ref[first_off + N_HEADS * nb * nk]
    bound = tab_ref[first_off + N_HEADS * nb * nk + 1]
    fast = bound < FOX_STATIC_MAX_LIMIT

    qn = _pair_rms(q_ref[...].astype(F32), gq_ref[...], *half) * scale
    cq = cq_ref[...]
    tile_i = i // 2
    row = lax.broadcasted_iota(jnp.int32, (tq, tk), 0) + (i % 2) * tq
    col = lax.broadcasted_iota(jnp.int32, (tq, tk), 1)
    causal = col <= row

    c_first = [tab_ref[(b * N_HEADS + 2 * p + h) * nq + i] for h in range(2)]

    def cond(j):
        jj = jnp.maximum(j, 0)
        need = [c_first[h] - tab_ref[first_off + (b * N_HEADS + 2 * p + h) * nk + jj] >= -thr for h in range(2)]
        return jnp.logical_and(j >= 0, jnp.logical_or(need[0], need[1]))

    def walk(tile):
        tile(tile_i, True)

        def body(j):
            tile(j, False)
            return j - 1
        lax.while_loop(cond, body, tile_i - 1)

    def q_aug(shift):
        return [jnp.where(half[h], qn, _aug_lanes(_lane_pick(cq, 2 * p + h) - shift, spare[h], False)).astype(BF16)
                for h in range(2)]

    @pl.when(fast)
    def _():
        qa = q_aug(bound)
        acc_ref[...] = jnp.zeros_like(acc_ref)

        def tile(j, masked):
            rows = pl.ds(pl.multiple_of(j * tk, tk), tk)
            s = [_dot_nt(qa[h], kaug_ref[h, rows, :]) for h in range(2)]
            if masked:
                s = [jnp.where(causal, sh, NEG_BIG) for sh in s]
            pr = [jnp.exp(sh).astype(BF16) for sh in s]
            for h in range(2):
                acc_ref[h] = acc_ref[h] + _dot(pr[h], vaug_ref[h, rows, :])

        walk(tile)
        out = [acc_ref[h] / acc_ref[h][:, spare[h]:spare[h] + 1] for h in range(2)]
        o_ref[...] = jnp.where(half[0], out[0], out[1]).astype(o_ref.dtype)

    @pl.when(jnp.logical_not(fast))
    def _():
        qa = q_aug(0.0)
        acc_ref[...] = jnp.zeros_like(acc_ref)
        m_ref[...] = jnp.full_like(m_ref, NEG_BIG)
        l_ref[...] = jnp.zeros_like(l_ref)

        def tile(j, masked):
            rows = pl.ds(pl.multiple_of(j * tk, tk), tk)
            for h in range(2):
                s = _dot_nt(qa[h], kaug_ref[h, rows, :])
                if masked:
                    s = jnp.where(causal, s, NEG_BIG)
                m_prev = m_ref[h]
                m_new = jnp.maximum(m_prev, jnp.max(s, axis=1, keepdims=True))
                alpha = jnp.exp(m_prev - m_new)
                pr = jnp.exp(s - m_new)
                l_ref[h] = alpha * l_ref[h] + jnp.sum(pr, axis=1, keepdims=True)
                acc_ref[h] = alpha * acc_ref[h] + _dot(pr.astype(BF16), vaug_ref[h, rows, :])
                m_ref[h] = m_new

        walk(tile)
        out = [acc_ref[h] / l_ref[h] for h in range(2)]
        o_ref[...] = jnp.where(half[0], out[0], out[1]).astype(o_ref.dtype)


def _fox_attention(big, c, tab, gq2, gk2, batch, seq, tq=256):
    nq = seq // tq
    t = batch * seq
    kern = functools.partial(_fox_kernel, seq=seq, tq=tq, tk=2 * tq, rows_per_pass=min(512, seq))
    grid_spec = pltpu.PrefetchScalarGridSpec(
        num_scalar_prefetch=1,
        grid=(batch, N_HEADS // 2, nq),
        in_specs=[
            pl.BlockSpec((tq, LANES), lambda b, p, i, tab: (b * nq + i, C_FOX + p)),
            pl.BlockSpec((seq, LANES), lambda b, p, i, tab: (b, C_FOX + 4 + p)),
            pl.BlockSpec((seq, LANES), lambda b, p, i, tab: (b, C_FOX + 8 + p)),
            pl.BlockSpec((tq, LANES), lambda b, p, i, tab: (b * nq + i, 0)),
            pl.BlockSpec((seq, LANES), lambda b, p, i, tab: (b, 0)),
            pl.BlockSpec((1, LANES), lambda b, p, i, tab: (0, 0)),
            pl.BlockSpec((1, LANES), lambda b, p, i, tab: (0, 0)),
        ],
        out_specs=pl.BlockSpec((tq, LANES), lambda b, p, i, tab: (b * nq + i, p)),
        scratch_shapes=[
            pltpu.VMEM((2, seq, LANES), BF16),
            pltpu.VMEM((2, seq, LANES), BF16),
            pltpu.VMEM((2, tq, LANES), F32),
            pltpu.VMEM((2, tq, 1), F32),
            pltpu.VMEM((2, tq, 1), F32),
        ],
    )
    return pl.pallas_call(
        kern,
        grid_spec=grid_spec,
        out_shape=jax.ShapeDtypeStruct((t, BRANCH_WIDTH), BF16),
        compiler_params=_params(("arbitrary", "arbitrary", "arbitrary")),
        name="fox_attention",
    )(tab, big, big, big, c, c, gq2, gk2)


def _sb_kernel(q_ref, k_ref, v_ref, gq_ref, gk_ref, upper_ref, o_ref,
               kn_ref, acc_ref, run_ref, flag_ref, *, seq, tq, rows_per_pass):
    i = pl.program_id(2)
    half = _half_masks()
    scale = HEAD_DIM ** -0.5

    @pl.when(i == 0)
    def _():
        def fill(r, carry):
            rows = pl.ds(pl.multiple_of(r * rows_per_pass, rows_per_pass), rows_per_pass)
            kn_ref[rows, :] = _pair_rms(k_ref[rows, :].astype(F32), gk_ref[...], *half).astype(BF16)
            return carry
        lax.fori_loop(0, seq // rows_per_pass, fill, 0)

    qn = _pair_rms(q_ref[...].astype(F32), gq_ref[...], *half) * scale
    qh = [jnp.where(half[h], qn, 0.0).astype(BF16) for h in range(2)]
    acc_ref[...] = jnp.zeros_like(acc_ref)
    run_ref[...] = jnp.zeros_like(run_ref)

    def step(start, width, offset):
        rows = pl.ds(pl.multiple_of(start, tq), width)
        kk, vv = kn_ref[rows, :], v_ref[rows, :]
        up = upper_ref[:width, :width]
        z = [_dot_nt(qh[h], kk) for h in range(2)]
        log_beta = [_log_sigmoid(zh) for zh in z]
        log_rest = [lb - zh for lb, zh in zip(log_beta, z)]
        if offset is not None:
            row = lax.broadcasted_iota(jnp.int32, (tq, width), 0) + offset
            col = lax.broadcasted_iota(jnp.int32, (tq, width), 1)
            past = col < row
            log_rest = [jnp.where(past, lr, 0.0) for lr in log_rest]
        parts = [_split2(lr) for lr in log_rest]
        inner = [_dot(hi, up) + _dot(lo, up) for hi, lo in parts]
        w = [jnp.exp(log_beta[h] + inner[h] + run_ref[h]) for h in range(2)]
        if offset is not None:
            w = [jnp.where(past, wh, 0.0) for wh in w]
        for h in range(2):
            acc_ref[h] = acc_ref[h] + _dot(w[h].astype(BF16), vv)
        run = [run_ref[h] + jnp.sum(log_rest[h], axis=1, keepdims=True) for h in range(2)]
        for h in range(2):
            run_ref[h] = run[h]
        flag_ref[0] = jnp.maximum(jnp.max(run[0]), jnp.max(run[1]))

    first_block = jnp.maximum(i - 1, 0)
    step(first_block * tq, 2 * tq, (i - first_block) * tq)

    def cond(j):
        return jnp.logical_and(j >= 0, flag_ref[0] > -EXP_UNDERFLOW)

    def body(j):
        step(j * tq, tq, None)
        return j - 1

    lax.while_loop(cond, body, first_block - 1)
    o_ref[...] = jnp.where(half[0], acc_ref[0], acc_ref[1]).astype(o_ref.dtype)


def _sb_attention(big, gq2, gk2, batch, seq, tq=256):
    nq = seq // tq
    t = batch * seq
    upper = jnp.asarray(np.tril(np.ones((2 * tq, 2 * tq), np.float32), -1), BF16)
    kern = functools.partial(_sb_kernel, seq=seq, tq=tq, rows_per_pass=min(512, seq))
    return pl.pallas_call(
        kern,
        grid=(batch, N_HEADS // 2, nq),
        in_specs=[
            pl.BlockSpec((tq, LANES), lambda b, p, i: (b * nq + i, C_SB + p)),
            pl.BlockSpec((seq, LANES), lambda b, p, i: (b, C_SB + 4 + p)),
            pl.BlockSpec((seq, LANES), lambda b, p, i: (b, C_SB + 8 + p)),
            pl.BlockSpec((1, LANES), lambda b, p, i: (0, 0)),
            pl.BlockSpec((1, LANES), lambda b, p, i: (0, 0)),
            pl.BlockSpec((2 * tq, 2 * tq), lambda b, p, i: (0, 0)),
        ],
        out_specs=pl.BlockSpec((tq, LANES), lambda b, p, i: (b * nq + i, p)),
        out_shape=jax.ShapeDtypeStruct((t, BRANCH_WIDTH), BF16),
        scratch_shapes=[
            pltpu.VMEM((seq, LANES), BF16),
            pltpu.VMEM((2, tq, LANES), F32),
            pltpu.VMEM((2, tq, 1), F32),
            pltpu.SMEM((1,), F32),
        ],
        compiler_params=_params(("arbitrary", "arbitrary", "arbitrary")),
        name="sb_attention",
    )(big, big, big, gq2, gk2, upper)


def _gdn_prep_kernel(xq_ref, xk_ref, xv_ref, hq_ref, hk_ref, hv_ref, small_ref, cw_ref, alog_ref, dtb_ref,
                     grp_ref, eg_ref, eb_ref, tri_ref, q_ref, k_ref, v_ref, gc_ref, beta_ref, *, tiles_per_seq):
    first = (pl.program_id(0) % tiles_per_seq) == 0

    def conv_silu(x_ref, halo_ref, sec):
        halo = jnp.where(first, 0.0, halo_ref[...].astype(F32))
        xx = jnp.concatenate([halo, x_ref[...].astype(F32)], axis=0)
        hr = halo.shape[0]
        cols = slice(sec * BRANCH_WIDTH, (sec + 1) * BRANCH_WIDTH)
        y = cw_ref[GDN_CONV - 1:GDN_CONV, cols] * xx[hr:, :]
        for d in range(1, GDN_CONV):
            y = y + cw_ref[GDN_CONV - 1 - d:GDN_CONV - d, cols] * pltpu.roll(xx, d, 0)[hr:, :]
        return _silu(y)

    def l2(u):
        ss = _dot_sel(u * u, grp_ref[...])
        return u * lax.rsqrt(ss + NORM_EPS)

    q_ref[...] = l2(conv_silu(xq_ref, hq_ref, 0)) * (HEAD_DIM ** -0.5)
    k_ref[...] = l2(conv_silu(xk_ref, hk_ref, 1))
    v_ref[...] = conv_silu(xv_ref, hv_ref, 2)

    sm = small_ref[...]
    g = -jnp.exp(alog_ref[...]) * _softplus(sm + dtb_ref[...])
    g_wide = _dot_sel(g, eg_ref[...])
    gc_ref[...] = _sel_dot(tri_ref[...], g_wide)
    beta_ref[...] = _dot_sel(_sigmoid(sm), eb_ref[...])


def _gdn_prep(big, small, conv_w, alog_row, dtb_row, batch, seq, tm=256):
    t = batch * seq
    halo_rows = 16
    grp = np.kron(np.eye(N_HEADS, dtype=np.float32), np.ones((HEAD_DIM, HEAD_DIM), np.float32))
    eg = np.zeros((LANES, BRANCH_WIDTH), np.float32)
    eb = np.zeros((LANES, BRANCH_WIDTH), np.float32)
    for h in range(N_HEADS):
        eg[L_GDNA + h, h * HEAD_DIM:(h + 1) * HEAD_DIM] = 1.0
        eb[L_GDNB + h, h * HEAD_DIM:(h + 1) * HEAD_DIM] = 1.0
    tri = np.kron(np.eye(tm // CHUNK, dtype=np.float32), np.tril(np.ones((CHUNK, CHUNK), np.float32)))
    wide = jax.ShapeDtypeStruct((t, BRANCH_WIDTH), F32)
    ratio = tm // halo_rows
    kern = functools.partial(_gdn_prep_kernel, tiles_per_seq=seq // tm)
    const = lambda shape: pl.BlockSpec(shape, lambda i: (0, 0))
    out_spec = pl.BlockSpec((tm, BRANCH_WIDTH), lambda i: (i, 0))
    sec0 = C_GDN * LANES // BRANCH_WIDTH
    x_specs = [pl.BlockSpec((tm, BRANCH_WIDTH), functools.partial(lambda i, s: (i, sec0 + s), s=s))
               for s in range(3)]
    halo_specs = [pl.BlockSpec((halo_rows, BRANCH_WIDTH),
                               functools.partial(lambda i, s: (jnp.maximum(i * ratio - 1, 0), sec0 + s), s=s))
                  for s in range(3)]
    return pl.pallas_call(
        kern,
        grid=(t // tm,),
        in_specs=x_specs + halo_specs + [
            pl.BlockSpec((tm, LANES), lambda i: (i, 0)),
            const((GDN_CONV, 3 * BRANCH_WIDTH)),
            const((1, LANES)),
            const((1, LANES)),
            const((BRANCH_WIDTH, BRANCH_WIDTH)),
            const((LANES, BRANCH_WIDTH)),
            const((LANES, BRANCH_WIDTH)),
            const((tm, tm)),
        ],
        out_specs=[out_spec] * 5,
        out_shape=[wide] * 5,
        compiler_params=_params(("arbitrary",)),
        name="gdn_prep",
    )(big, big, big, big, big, big, small, conv_w, alog_row, dtb_row, jnp.asarray(grp, BF16),
      jnp.asarray(eg, BF16), jnp.asarray(eb, BF16), jnp.asarray(tri, BF16))


def _stack_heads(x, half):
    return jnp.concatenate([jnp.where(half[0], x, 0.0), jnp.where(half[1], x, 0.0)], axis=0)


def _unstack(x):
    return x[:CHUNK, :] + x[CHUNK:, :]


def _gdn_kernel(q_ref, k_ref, v_ref, gc_ref, beta_ref, gain_ref, o_ref, state_ref, *, chunks):
    @pl.when(pl.program_id(1) == 0)
    def _():
        state_ref[...] = jnp.zeros_like(state_ref)

    half = _half_masks()
    n2 = 2 * CHUNK
    r = lax.broadcasted_iota(jnp.int32, (n2, n2), 0)
    c = lax.broadcasted_iota(jnp.int32, (n2, n2), 1)
    same = (r < CHUNK) == (c < CHUNK)
    lower = same & (c <= r)
    strict = same & (c < r)
    eye = (r == c).astype(F32)
    lane = _lane_iota((1, LANES))
    sub = lane % HEAD_DIM

    pairs = range(N_HEADS // 2)
    cps = [(ci, p) for ci in range(chunks) for p in pairs]
    win = {cp: (slice(cp[0] * CHUNK, (cp[0] + 1) * CHUNK), slice(cp[1] * LANES, (cp[1] + 1) * LANES))
           for cp in cps}

    diff_l, diff_r, kb_s, q_s, kk2, rhs, k_dec, q_dec, last = {}, {}, {}, {}, {}, {}, {}, {}, {}
    for cp in cps:
        rows, cols = win[cp]
        q2, k2, v2 = q_ref[rows, cols], k_ref[rows, cols], v_ref[rows, cols]
        gc2, beta2 = gc_ref[rows, cols], beta_ref[rows, cols]
        hi, lo, lo2 = [u.astype(F32) for u in _split3(gc2)]
        one = jnp.where((sub >= 3) & (sub < 6), 1.0, 0.0)
        la = jnp.where(sub == 0, hi, jnp.where(sub == 1, lo, jnp.where(sub == 2, lo2, one)))
        one = jnp.where(sub < 3, 1.0, 0.0)
        ra = jnp.where(sub == 3, -hi, jnp.where(sub == 4, -lo, jnp.where(sub == 5, -lo2, one))).astype(BF16)
        diff_l[cp] = _stack_heads(la, half).astype(BF16)
        diff_r[cp] = jnp.concatenate([ra, ra], axis=0)
        eg = jnp.exp(gc2)
        kb = k2 * beta2
        kk = k2.astype(BF16)
        kk2[cp] = jnp.concatenate([kk, kk], axis=0)
        kb_s[cp] = _stack_heads(kb, half).astype(BF16)
        q_s[cp] = _stack_heads(q2, half).astype(BF16)
        rhs[cp] = jnp.concatenate([_stack_heads(v2 * beta2, half), _stack_heads(kb * eg, half)],
                                  axis=1).astype(BF16)
        k_dec[cp] = (k2 * jnp.exp(gc2[CHUNK - 1:CHUNK, :] - gc2)).astype(BF16)
        q_dec[cp] = q2 * eg
        last[cp] = eg[CHUNK - 1:CHUNK, :]

    diff = {cp: _dot_nt(diff_l[cp], diff_r[cp]) for cp in cps}
    akk = {cp: _dot_nt(kb_s[cp], kk2[cp]) for cp in cps}
    aqk = {cp: _dot_nt(q_s[cp], kk2[cp]) for cp in cps}

    x, pw, intra = {}, {}, {}
    for cp in cps:
        decay = jnp.where(lower, jnp.exp(jnp.minimum(diff[cp], 0.0)), 0.0)
        a = jnp.where(strict, akk[cp] * decay, 0.0)
        intra[cp] = (aqk[cp] * decay).astype(BF16)
        x[cp] = eye - a
        pw[cp] = a

    for _ in range(5):
        for cp in cps:
            pb = pw[cp].astype(BF16)
            pw[cp] = _dot(pb, pb)
        for cp in cps:
            x[cp] = x[cp] + _dot(x[cp].astype(BF16), pw[cp].astype(BF16))

    uw = {cp: _dot(x[cp].astype(BF16), rhs[cp]) for cp in cps}
    u2 = {cp: _unstack(uw[cp][:, :LANES]) for cp in cps}
    w2 = {cp: _unstack(uw[cp][:, LANES:]) for cp in cps}

    uw_stack = {cp: jnp.concatenate([_stack_heads(u2[cp], half), _stack_heads(w2[cp], half)],
                                    axis=1).astype(BF16) for cp in cps}
    uw_flat = {cp: jnp.concatenate([u2[cp], w2[cp]], axis=1).astype(BF16) for cp in cps}
    iuw = {cp: _dot(intra[cp], uw_stack[cp]) for cp in cps}
    kuw = {cp: _dot_tn(k_dec[cp], uw_flat[cp]) for cp in cps}
    o_loc, lhs, s_add = {}, {}, {}
    for cp in cps:
        o_loc[cp] = _unstack(iuw[cp][:, :LANES])
        q_eff = q_dec[cp] - _unstack(iuw[cp][:, LANES:])
        m_eff = jnp.where(same, -kuw[cp][:, LANES:], 0.0)
        lhs[cp] = jnp.concatenate([q_eff, m_eff], axis=0).astype(BF16)
        s_add[cp] = jnp.where(same, kuw[cp][:, :LANES], 0.0)

    state = {p: state_ref[p] for p in pairs}
    for ci in range(chunks):
        res = {}
        for p in pairs:
            hi, lo = _split2(state[p])
            res[p] = _dot(lhs[(ci, p)], jnp.concatenate([hi, lo], axis=1))
        for p in pairs:
            cp = (ci, p)
            rr = res[p][:, :LANES] + res[p][:, LANES:]
            o2 = rr[:CHUNK, :] + o_loc[cp]
            state[p] = state[p] * last[cp] + rr[CHUNK:, :] + s_add[cp]
            rows, cols = win[cp]
            o_ref[rows, cols] = _pair_rms(o2, gain_ref[...], *half).astype(o_ref.dtype)
    for p in pairs:
        state_ref[p] = state[p]


def _gdn_scan(qn, kn, vv, gc, beta, gain2, batch, seq, chunks=2):
    t = batch * seq
    tm = chunks * CHUNK
    nb = seq // tm
    spec = pl.BlockSpec((tm, BRANCH_WIDTH), lambda b, n: (b * nb + n, 0))
    return pl.pallas_call(
        functools.partial(_gdn_kernel, chunks=chunks),
        grid=(batch, nb),
        in_specs=[spec] * 5 + [pl.BlockSpec((1, LANES), lambda b, n: (0, 0))],
        out_specs=spec,
        out_shape=jax.ShapeDtypeStruct((t, BRANCH_WIDTH), BF16),
        scratch_shapes=[pltpu.VMEM((N_HEADS // 2, LANES, LANES), F32)],
        compiler_params=_params(("arbitrary", "arbitrary")),
        name="gdn_scan",
    )(qn, kn, vv, gc, beta, gain2)


GLA_LEVELS = 6


def _gla_constants():
    n = CHUNK
    t = np.arange(n)
    blocks = []
    masks = []
    for lev in range(GLA_LEVELS):
        w = 1 << lev
        lower = (t // w) % 2 == 1
        mid_q = (t // w) * w
        cq = (lower[:, None] & (t[None, :] > mid_q[:, None]) & (t[None, :] <= t[:, None])).astype(np.float32)
        mid_k = (t // w + 1) * w
        ck = ((~lower)[:, None] & (t[None, :] > t[:, None]) & (t[None, :] <= mid_k[:, None])).astype(np.float32)
        blocks += [cq, ck]
        m = lower[:, None] & (~lower)[None, :] & ((t // (2 * w))[:, None] == (t // (2 * w))[None, :])
        masks.append(np.kron(np.eye(2), m.astype(np.float32)))
    masks.append(np.eye(2 * n, dtype=np.float32))
    tri = np.tril(np.ones((n, n), np.float32))
    rev = (t[None, :] > t[:, None]).astype(np.float32)
    blocks += [tri, rev]
    return np.concatenate(blocks, axis=0), np.stack(masks)


def _gla_kernel(x_ref, small_ref, w2_ref, bias_ref, sel_ref, mask_ref, gain_ref, o_ref, state_ref, *, chunks):
    @pl.when(pl.program_id(1) == 0)
    def _():
        state_ref[...] = jnp.zeros_like(state_ref)

    half = _half_masks()
    qk_w = GLA_HEADS * HEAD_DIM
    r = lax.broadcasted_iota(jnp.int32, (2 * LANES, LANES), 0)
    c = lax.broadcasted_iota(jnp.int32, (2 * LANES, LANES), 1)
    state_mask = (r < LANES) == (c < HEAD_DIM)
    zeros_v = jnp.zeros((CHUNK, LANES), F32)

    pairs = range(GLA_HEADS // 2)
    cps = [(ci, p) for ci in range(chunks) for p in pairs]
    crow = lambda ci: slice(ci * CHUNK, (ci + 1) * CHUNK)

    log_a = {}
    for ci in range(chunks):
        y = _dot(small_ref[crow(ci), :].astype(BF16), w2_ref[...]) + bias_ref[...]
        log_a[ci] = _log_sigmoid(y) * (1.0 / GLA_GATE_TAU)

    q2, k2, v_pair, la_split = {}, {}, {}, {}
    for ci, p in cps:
        rows = crow(ci)
        q2[ci, p] = x_ref[rows, p * LANES:(p + 1) * LANES].astype(F32) * (HEAD_DIM ** -0.5)
        k2[ci, p] = x_ref[rows, qk_w + p * LANES:qk_w + (p + 1) * LANES].astype(F32)
        v_pair[ci, p] = x_ref[rows, 2 * qk_w + 2 * p * LANES:2 * qk_w + 2 * (p + 1) * LANES]
        hi, lo = _split2(log_a[ci][:, p * LANES:(p + 1) * LANES])
        la_split[ci, p] = jnp.concatenate([hi, lo], axis=1)

    sums = {}
    for cp in cps:
        s2 = _dot(sel_ref[...], la_split[cp])
        sums[cp] = s2[:, :LANES] + s2[:, LANES:]

    a = {cp: None for cp in cps}
    for lev in range(GLA_LEVELS + 1):
        lhs, rhs = {}, {}
        for cp in cps:
            if lev < GLA_LEVELS:
                dq = sums[cp][(2 * lev) * CHUNK:(2 * lev + 1) * CHUNK, :]
                dk = sums[cp][(2 * lev + 1) * CHUNK:(2 * lev + 2) * CHUNK, :]
                ql, kl = q2[cp] * jnp.exp(dq), k2[cp] * jnp.exp(dk)
            else:
                ql, kl = q2[cp], k2[cp]
            kb = kl.astype(BF16)
            lhs[cp] = _stack_heads(ql, half).astype(BF16)
            rhs[cp] = jnp.concatenate([kb, kb], axis=0)
        prod = {cp: _dot_nt(lhs[cp], rhs[cp]) for cp in cps}
        for cp in cps:
            term = prod[cp] * mask_ref[lev]
            a[cp] = term if a[cp] is None else a[cp] + term

    v_stack, q_dec, k_dec, b_last = {}, {}, {}, {}
    for cp in cps:
        b = sums[cp][2 * GLA_LEVELS * CHUNK:(2 * GLA_LEVELS + 1) * CHUNK, :]
        b_rev = sums[cp][(2 * GLA_LEVELS + 1) * CHUNK:, :]
        vf = v_pair[cp].astype(F32)
        v_stack[cp] = jnp.concatenate(
            [jnp.concatenate([vf[:, :LANES], zeros_v], axis=1),
             jnp.concatenate([zeros_v, vf[:, LANES:]], axis=1)], axis=0).astype(BF16)
        q_dec[cp] = (q2[cp] * jnp.exp(b)).astype(BF16)
        k_dec[cp] = (k2[cp] * jnp.exp(b_rev)).astype(BF16)
        b_last[cp] = jnp.exp(b[CHUNK - 1:CHUNK, :])
    o_loc = {cp: _unstack(_dot(a[cp].astype(BF16), v_stack[cp])) for cp in cps}
    upd = {cp: jnp.where(state_mask, _dot_tn(v_pair[cp], k_dec[cp]), 0.0) for cp in cps}

    state = {p: state_ref[p] for p in pairs}
    for ci in range(chunks):
        o_pair = {p: o_loc[ci, p] + _dot_nt(q_dec[ci, p], state[p].astype(BF16)) for p in pairs}
        for p in pairs:
            state[p] = state[p] * b_last[ci, p] + upd[ci, p]
            for h in range(2):
                oh = o_pair[p][:, h * LANES:(h + 1) * LANES]
                ms = jnp.mean(oh * oh, axis=-1, keepdims=True)
                col0 = (2 * p + h) * LANES
                o_ref[crow(ci), col0:col0 + LANES] = (
                    oh * lax.rsqrt(ms + NORM_EPS) * gain_ref[...]).astype(o_ref.dtype)
    for p in pairs:
        state_ref[p] = state[p]


def _gla(big, small, w2_ext, bias_row, gain_row, batch, seq, chunks=2):
    t = batch * seq
    tm = chunks * CHUNK
    nb = seq // tm
    sel, masks = _gla_constants()
    width = 2 * GLA_HEADS * HEAD_DIM + BRANCH_WIDTH
    const2 = lambda shape: pl.BlockSpec(shape, lambda b, n: (0, 0))
    return pl.pallas_call(
        functools.partial(_gla_kernel, chunks=chunks),
        grid=(batch, nb),
        in_specs=[
            pl.BlockSpec((tm, width), lambda b, n: (b * nb + n, C_GLA * LANES // width)),
            pl.BlockSpec((tm, LANES), lambda b, n: (b * nb + n, 0)),
            const2((LANES, GLA_HEADS * HEAD_DIM)),
            const2((1, GLA_HEADS * HEAD_DIM)),
            const2(sel.shape),
            pl.BlockSpec(masks.shape, lambda b, n: (0, 0, 0)),
            const2((1, LANES)),
        ],
        out_specs=pl.BlockSpec((tm, BRANCH_WIDTH), lambda b, n: (b * nb + n, 0)),
        out_shape=jax.ShapeDtypeStruct((t, BRANCH_WIDTH), BF16),
        scratch_shapes=[pltpu.VMEM((GLA_HEADS // 2, 2 * LANES, LANES), F32)],
        compiler_params=_params(("arbitrary", "arbitrary")),
        name="gla",
    )(big, small, w2_ext, bias_row, jnp.asarray(sel, BF16), jnp.asarray(masks, F32), gain_row)


def _merge_kernel(x_ref, o0_ref, o1_ref, o2_ref, o3_ref, z_ref, ml_ref, gb_ref, wb_ref, wo_ref, out_ref):
    merged = None
    for n, o_ref in enumerate((o0_ref, o1_ref, o2_ref, o3_ref)):
        z = z_ref[:, n * BRANCH_WIDTH:(n + 1) * BRANCH_WIDTH].astype(F32)
        br = (o_ref[...].astype(F32) * _silu(z)).astype(BF16)
        proj = _dot(br, wb_ref[n])
        gate = _sigmoid(ml_ref[:, n * D_MODEL:(n + 1) * D_MODEL].astype(F32)
                        + gb_ref[:, n * D_MODEL:(n + 1) * D_MODEL])
        term = proj * gate
        merged = term if merged is None else merged + term
    out_ref[...] = x_ref[...] + _dot(merged.astype(BF16), wo_ref[...])


def _merge(x2, outs, big, gate_bias_row, w_branch, w_out, tm=256):
    t = x2.shape[0]
    zw = N_BRANCH * BRANCH_WIDTH
    mw = N_BRANCH * D_MODEL
    o_spec = pl.BlockSpec((tm, BRANCH_WIDTH), lambda i: (i, 0))
    return pl.pallas_call(
        _merge_kernel,
        grid=(t // tm,),
        in_specs=[
            pl.BlockSpec((tm, D_MODEL), lambda i: (i, 0)),
            o_spec, o_spec, o_spec, o_spec,
            pl.BlockSpec((tm, zw), lambda i: (i, C_Z * LANES // zw)),
            pl.BlockSpec((tm, mw), lambda i: (i, C_MERGE * LANES // mw)),
            pl.BlockSpec((1, mw), lambda i: (0, 0)),
            pl.BlockSpec((N_BRANCH, BRANCH_WIDTH, D_MODEL), lambda i: (0, 0, 0)),
            pl.BlockSpec((D_MODEL, D_MODEL), lambda i: (0, 0)),
        ],
        out_specs=pl.BlockSpec((tm, D_MODEL), lambda i: (i, 0)),
        out_shape=jax.ShapeDtypeStruct((t, D_MODEL), F32),
        compiler_params=_params(("arbitrary",)),
        name="merge_out",
    )(x2, *outs, big, big, gate_bias_row, w_branch, w_out)


def _reorder_w_in(w_in):
    sizes = (3 * BRANCH_WIDTH, N_HEADS, BRANCH_WIDTH, 3 * BRANCH_WIDTH, BRANCH_WIDTH,
             3 * BRANCH_WIDTH, N_HEADS, N_HEADS, BRANCH_WIDTH,
             2 * GLA_HEADS * HEAD_DIM, BRANCH_WIDTH, GLA_GATE_RANK, BRANCH_WIDTH, N_BRANCH * D_MODEL)
    (fox_qkv, fox_f, fox_z, sb_qkv, sb_z, gdn_qkv, gdn_a, gdn_b, gdn_z,
     gla_qk, gla_v, gla_g, gla_z, merge) = jnp.split(w_in, np.cumsum(sizes)[:-1].tolist(), axis=1)
    big = jnp.concatenate([merge, fox_z, sb_z, gdn_z, gla_z, gla_qk, gla_v, gdn_qkv, fox_qkv, sb_qkv], axis=1)
    pad = jnp.zeros((w_in.shape[0], LANES - (3 * N_HEADS + GLA_GATE_RANK)), w_in.dtype)
    small = jnp.concatenate([fox_f, gdn_a, gdn_b, gla_g, pad], axis=1)
    return big.astype(BF16), small.astype(BF16)


def _lane_row(values, offset):
    row = jnp.zeros((1, LANES), F32)
    return row.at[0, offset:offset + values.shape[0]].set(values.astype(F32))


def _layer(x2, batch, seq, norm_g, w_in, fox_f_bias, fox_q_norm, fox_k_norm, sb_q_norm, sb_k_norm,
           gdn_conv_w, gdn_a_log, gdn_dt_bias, gdn_o_norm, gla_gate_w2, gla_gate_bias, gla_o_norm,
           gate_bias, w_branch, w_out):
    w_big, w_small = _reorder_w_in(w_in)
    g_row = norm_g.reshape(1, D_MODEL).astype(F32)
    t = batch * seq
    big = _norm_proj(x2, g_row, w_big, BF16, min(2048, t), 512)
    small = _norm_proj(x2, g_row, w_small, F32, min(1024, t), LANES)

    pair = lambda g: jnp.tile(g.astype(F32), 2).reshape(1, LANES)

    c = _fox_cum(small, _lane_row(fox_f_bias, L_FOXF), batch, seq)
    tq = min(256, seq)
    nq = seq // tq
    c3 = c.reshape(batch, seq, LANES)[:, :, :N_HEADS]
    tk = 2 * tq
    first = jnp.transpose(c3[:, 0::tq, :], (0, 2, 1))
    last = jnp.transpose(c3[:, tk - 1::tk, :], (0, 2, 1))
    qk_bound = 1.02 * HEAD_DIM ** 0.5 * jnp.max(jnp.abs(fox_q_norm)) * jnp.max(jnp.abs(fox_k_norm))
    thr = EXP_UNDERFLOW + 2.0 * qk_bound + 1.0
    tab = jnp.concatenate([first.reshape(-1), last.reshape(-1), thr.reshape(1), qk_bound.reshape(1)]).astype(F32)
    o_fox = _fox_attention(big, c, tab, pair(fox_q_norm), pair(fox_k_norm), batch, seq, tq)

    o_sb = _sb_attention(big, pair(sb_q_norm), pair(sb_k_norm), batch, seq, tq)

    qn, kn, vv, gc, beta = _gdn_prep(big, small, gdn_conv_w.astype(F32), _lane_row(gdn_a_log, L_GDNA),
                                     _lane_row(gdn_dt_bias, L_GDNA), batch, seq, min(256, seq))
    o_gdn = _gdn_scan(qn, kn, vv, gc, beta, pair(gdn_o_norm), batch, seq)

    w2_ext = jnp.zeros((LANES, GLA_HEADS * HEAD_DIM), F32).at[L_GLAG:L_GLAG + GLA_GATE_RANK].set(
        gla_gate_w2.astype(F32)).astype(BF16)
    o_gla = _gla(big, small, w2_ext, gla_gate_bias.reshape(1, -1).astype(F32),
                 gla_o_norm.reshape(1, LANES).astype(F32), batch, seq)

    return _merge(x2, (o_fox, o_sb, o_gdn, o_gla), big, gate_bias.reshape(1, -1).astype(F32),
                  w_branch.astype(BF16), w_out.astype(BF16), min(256, t))


def kernel(x, norm_g, w_in, fox_f_bias, fox_q_norm, fox_k_norm, sb_q_norm, sb_k_norm, gdn_conv_w,
           gdn_a_log, gdn_dt_bias, gdn_o_norm, gla_gate_w2, gla_gate_bias, gla_o_norm, gate_bias,
           w_branch, w_out):
    batch, seq, d = x.shape
    x2 = x.reshape(batch * seq, d)
    params = (norm_g, w_in, fox_f_bias, fox_q_norm, fox_k_norm, sb_q_norm, sb_k_norm, gdn_conv_w,
              gdn_a_log, gdn_dt_bias, gdn_o_norm, gla_gate_w2, gla_gate_bias, gla_o_norm, gate_bias,
              w_branch, w_out)
    for l in range(norm_g.shape[0]):
        x2 = _layer(x2, batch, seq, *[p[l] for p in params])
    return x2.reshape(batch, seq, d)
```

```python
import functools

import numpy as np
import jax
import jax.numpy as jnp
from jax import lax
from jax.experimental import pallas as pl
from jax.experimental.pallas import tpu as pltpu

F32 = jnp.float32
BF16 = jnp.bfloat16

LANES = 128
HEAD_DIM = 64
D_MODEL = 1024
BRANCH_WIDTH = 512
N_BRANCH = 4
N_HEADS = 8
GLA_HEADS = 4
GLA_GATE_RANK = 16
GLA_GATE_TAU = 16.0
GDN_CONV = 4
CHUNK = 64
NORM_EPS = 1e-6
VMEM_LIMIT = 48 * 1024 * 1024
EXP_UNDERFLOW = 104.0
NEG_BIG = -1e30

C_MERGE, C_Z, C_GLA, C_GDN, C_FOX, C_SB, C_END = 0, 32, 48, 56, 68, 80, 92
N_BIG = C_END * LANES
L_FOXF, L_GDNA, L_GDNB, L_GLAG = 0, 8, 16, 24


def _dot(a, b):
    return jnp.dot(a, b, preferred_element_type=F32)


def _dot_nt(a, b):
    return lax.dot_general(a, b, (((1,), (1,)), ((), ())), preferred_element_type=F32)


def _dot_tn(a, b):
    return lax.dot_general(a, b, (((0,), (0,)), ((), ())), preferred_element_type=F32)


def _split2(x):
    hi = x.astype(BF16)
    lo = (x - hi.astype(F32)).astype(BF16)
    return hi, lo


def _split3(x):
    hi = x.astype(BF16)
    r = x - hi.astype(F32)
    lo = r.astype(BF16)
    lo2 = (r - lo.astype(F32)).astype(BF16)
    return hi, lo, lo2


def _sel_dot(sel, x):
    hi, lo, lo2 = _split3(x)
    return _dot(sel, hi) + _dot(sel, lo) + _dot(sel, lo2)


def _dot_sel(x, sel):
    hi, lo, lo2 = _split3(x)
    return _dot(hi, sel) + _dot(lo, sel) + _dot(lo2, sel)


def _softplus(x):
    return jnp.maximum(x, 0.0) + jnp.log(1.0 + jnp.exp(-jnp.abs(x)))


def _log_sigmoid(x):
    return jnp.minimum(x, 0.0) - jnp.log(1.0 + jnp.exp(-jnp.abs(x)))


def _sigmoid(x):
    return 1.0 / (1.0 + jnp.exp(-x))


def _silu(x):
    return x / (1.0 + jnp.exp(-x))


def _lane_iota(shape):
    return lax.broadcasted_iota(jnp.int32, shape, len(shape) - 1)


def _half_masks():
    lane = _lane_iota((1, LANES))
    return lane < HEAD_DIM, lane >= HEAD_DIM


def _params(semantics):
    return pltpu.CompilerParams(dimension_semantics=semantics, vmem_limit_bytes=VMEM_LIMIT)


def _proj_kernel(x_ref, g_ref, w_ref, o_ref, h_ref):
    @pl.when(pl.program_id(1) == 0)
    def _():
        x = x_ref[...]
        ms = jnp.mean(x * x, axis=-1, keepdims=True)
        h_ref[...] = (x * lax.rsqrt(ms + NORM_EPS) * g_ref[...]).astype(BF16)

    o_ref[...] = _dot(h_ref[...], w_ref[...]).astype(o_ref.dtype)


def _norm_proj(x2, g, w, out_dtype, tm, tn):
    t, d = x2.shape
    n = w.shape[1]
    return pl.pallas_call(
        _proj_kernel,
        grid=(t // tm, n // tn),
        in_specs=[
            pl.BlockSpec((tm, d), lambda i, j: (i, 0)),
            pl.BlockSpec((1, d), lambda i, j: (0, 0)),
            pl.BlockSpec((d, tn), lambda i, j: (0, j)),
        ],
        out_specs=pl.BlockSpec((tm, tn), lambda i, j: (i, j)),
        out_shape=jax.ShapeDtypeStruct((t, n), out_dtype),
        scratch_shapes=[pltpu.VMEM((tm, d), BF16)],
        compiler_params=_params(("arbitrary", "arbitrary")),
        name="norm_proj",
    )(x2, g, w)


def _fox_cum_kernel(f_ref, bias_ref, tri_ref, c_ref, carry_ref):
    @pl.when(pl.program_id(1) == 0)
    def _():
        carry_ref[...] = jnp.zeros_like(carry_ref)

    lf = _log_sigmoid(f_ref[...] + bias_ref[...])
    cs = _sel_dot(tri_ref[...], lf) + carry_ref[...]
    c_ref[...] = cs
    carry_ref[...] = cs[-1:, :]


def _fox_cum(small, bias_row, batch, seq, tr=512):
    tri = jnp.asarray(np.tril(np.ones((tr, tr), np.float32)), BF16)
    nb = seq // tr
    return pl.pallas_call(
        _fox_cum_kernel,
        grid=(batch, nb),
        in_specs=[
            pl.BlockSpec((tr, LANES), lambda b, r: (b * nb + r, 0)),
            pl.BlockSpec((1, LANES), lambda b, r: (0, 0)),
            pl.BlockSpec((tr, tr), lambda b, r: (0, 0)),
        ],
        out_specs=pl.BlockSpec((tr, LANES), lambda b, r: (b * nb + r, 0)),
        out_shape=jax.ShapeDtypeStruct(small.shape, F32),
        scratch_shapes=[pltpu.VMEM((1, LANES), F32)],
        compiler_params=_params(("arbitrary", "arbitrary")),
        name="fox_cum",
    )(small, bias_row, tri)


def _pair_rms(x, gain, half0, half1):
    sq = x * x
    ms0 = jnp.sum(jnp.where(half0, sq, 0.0), axis=-1, keepdims=True) * (1.0 / HEAD_DIM)
    ms1 = jnp.sum(jnp.where(half1, sq, 0.0), axis=-1, keepdims=True) * (1.0 / HEAD_DIM)
    r = jnp.where(half0, lax.rsqrt(ms0 + NORM_EPS), lax.rsqrt(ms1 + NORM_EPS))
    return x * r * gain


def _lane_pick(x, idx):
    lane = _lane_iota((1, LANES))
    return jnp.sum(jnp.where(lane == idx, x, 0.0), axis=-1, keepdims=True)


def _aug_lanes(c_col, base, key_side):
    lane = _lane_iota((1, LANES))
    hi, lo, lo2 = [v.astype(F32) for v in _split3(c_col)]
    if key_side:
        hi, lo, lo2 = -hi, -lo, -lo2
        vals, ones = (base + 3, base + 4, base + 5), (base, base + 1, base + 2)
    else:
        vals, ones = (base, base + 1, base + 2), (base + 3, base + 4, base + 5)
    out = jnp.where((lane == ones[0]) | (lane == ones[1]) | (lane == ones[2]), 1.0, 0.0)
    out = jnp.where(lane == vals[0], hi, out)
    out = jnp.where(lane == vals[1], lo, out)
    out = jnp.where(lane == vals[2], lo2, out)
    return out


FOX_STATIC_MAX_LIMIT = 20.0


def _fox_kernel(tab_ref, q_ref, k_ref, v_ref, cq_ref, ck_ref, gq_ref, gk_ref, o_ref,
                kaug_ref, vaug_ref, acc_ref, m_ref, l_ref, *, seq, tq, tk, rows_per_pass):
    b, p, i = pl.program_id(0), pl.program_id(1), pl.program_id(2)
    nq, nk = seq // tq, seq // tk
    nb = pl.num_programs(0)
    half = _half_masks()
    lane = _lane_iota((1, LANES))
    scale = HEAD_DIM ** -0.5
    spare = [HEAD_DIM * (1 - h) for h in range(2)]

    @pl.when(i == 0)
    def _():
        def fill(r, carry):
            rows = pl.ds(pl.multiple_of(r * rows_per_pass, rows_per_pass), rows_per_pass)
            kn = _pair_rms(k_ref[rows, :].astype(F32), gk_ref[...], *half)
            cc = ck_ref[rows, :]
            vv = v_ref[rows, :]
            for h in range(2):
                aug = _aug_lanes(_lane_pick(cc, 2 * p + h), spare[h], True)
                kaug_ref[h, rows, :] = jnp.where(half[h], kn, aug).astype(BF16)
                ones = jnp.where(lane == spare[h], 1.0, 0.0).astype(BF16)
                vaug_ref[h, rows, :] = jnp.where(half[h], vv, ones)
            return carry
        lax.fori_loop(0, seq // rows_per_pass, fill, 0)

    first_off = N_HEADS * nb * nq
    thr = tab_ref[first_off + N_HEADS * nb * nk]
    bound = tab_ref[first_off + N_HEADS * nb * nk + 1]
    fast = bound < FOX_STATIC_MAX_LIMIT

    qn = _pair_rms(q_ref[...].astype(F32), gq_ref[...], *half) * scale
    cq = cq_ref[...]
    tile_i = i // (tk // tq)
    row = lax.broadcasted_iota(jnp.int32, (tq, tk), 0) + (i % (tk // tq)) * tq
    col = lax.broadcasted_iota(jnp.int32, (tq, tk), 1)
    causal = col <= row

    c_first = [tab_ref[(b * N_HEADS + 2 * p + h) * nq + i] for h in range(2)]

    def cond(j):
        jj = jnp.maximum(j, 0)
        need = [c_first[h] - tab_ref[first_off + (b * N_HEADS + 2 * p + h) * nk + jj] >= -thr for h in range(2)]
        return jnp.logical_and(j >= 0, jnp.logical_or(need[0], need[1]))

    def walk(tile):
        tile(tile_i, True)

        def body(j):
            tile(j, False)
            return j - 1
        lax.while_loop(cond, body, tile_i - 1)

    def q_aug(shift):
        return [jnp.where(half[h], qn, _aug_lanes(_lane_pick(cq, 2 * p + h) - shift, spare[h], False)).astype(BF16)
                for h in range(2)]

    @pl.when(fast)
    def _():
        qa = q_aug(bound)
        acc_ref[...] = jnp.zeros_like(acc_ref)

        def tile(j, masked):
            rows = pl.ds(pl.multiple_of(j * tk, tk), tk)
            s = [_dot_nt(qa[h], kaug_ref[h, rows, :]) for h in range(2)]
            if masked:
                s = [jnp.where(causal, sh, NEG_BIG) for sh in s]
            pr = [jnp.exp(sh).astype(BF16) for sh in s]
            for h in range(2):
                acc_ref[h] = acc_ref[h] + _dot(pr[h], vaug_ref[h, rows, :])

        walk(tile)
        out = [acc_ref[h] / acc_ref[h][:, spare[h]:spare[h] + 1] for h in range(2)]
        o_ref[...] = jnp.where(half[0], out[0], out[1]).astype(o_ref.dtype)

    @pl.when(jnp.logical_not(fast))
    def _():
        qa = q_aug(0.0)
        acc_ref[...] = jnp.zeros_like(acc_ref)
        m_ref[...] = jnp.full_like(m_ref, NEG_BIG)
        l_ref[...] = jnp.zeros_like(l_ref)

        def tile(j, masked):
            rows = pl.ds(pl.multiple_of(j * tk, tk), tk)
            for h in range(2):
                s = _dot_nt(qa[h], kaug_ref[h, rows, :])
                if masked:
                    s = jnp.where(causal, s, NEG_BIG)
                m_prev = m_ref[h]
                m_new = jnp.maximum(m_prev, jnp.max(s, axis=1, keepdims=True))
                alpha = jnp.exp(m_prev - m_new)
                pr = jnp.exp(s - m_new)
                l_ref[h] = alpha * l_ref[h] + jnp.sum(pr, axis=1, keepdims=True)
                acc_ref[h] = alpha * acc_ref[h] + _dot(pr.astype(BF16), vaug_ref[h, rows, :])
                m_ref[h] = m_new

        walk(tile)
        out = [acc_ref[h] / l_ref[h] for h in range(2)]
        o_ref[...] = jnp.where(half[0], out[0], out[1]).astype(o_ref.dtype)


def _fox_attention(big, c, tab, gq2, gk2, batch, seq, tq, tk):
    nq = seq // tq
    t = batch * seq
    kern = functools.partial(_fox_kernel, seq=seq, tq=tq, tk=tk, rows_per_pass=min(512, seq))
    grid_spec = pltpu.PrefetchScalarGridSpec(
        num_scalar_prefetch=1,
        grid=(batch, N_HEADS // 2, nq),
        in_specs=[
            pl.BlockSpec((tq, LANES), lambda b, p, i, tab: (b * nq + i, C_FOX + p)),
            pl.BlockSpec((seq, LANES), lambda b, p, i, tab: (b, C_FOX + 4 + p)),
            pl.BlockSpec((seq, LANES), lambda b, p, i, tab: (b, C_FOX + 8 + p)),
            pl.BlockSpec((tq, LANES), lambda b, p, i, tab: (b * nq + i, 0)),
            pl.BlockSpec((seq, LANES), lambda b, p, i, tab: (b, 0)),
            pl.BlockSpec((1, LANES), lambda b, p, i, tab: (0, 0)),
            pl.BlockSpec((1, LANES), lambda b, p, i, tab: (0, 0)),
        ],
        out_specs=pl.BlockSpec((tq, LANES), lambda b, p, i, tab: (b * nq + i, p)),
        scratch_shapes=[
            pltpu.VMEM((2, seq, LANES), BF16),
            pltpu.VMEM((2, seq, LANES), BF16),
            pltpu.VMEM((2, tq, LANES), F32),
            pltpu.VMEM((2, tq, 1), F32),
            pltpu.VMEM((2, tq, 1), F32),
        ],
    )
    return pl.pallas_call(
        kern,
        grid_spec=grid_spec,
        out_shape=jax.ShapeDtypeStruct((t, BRANCH_WIDTH), BF16),
        compiler_params=_params(("arbitrary", "arbitrary", "arbitrary")),
        name="fox_attention",
    )(tab, big, big, big, c, c, gq2, gk2)


def _sb_kernel(q_ref, k_ref, v_ref, gq_ref, gk_ref, upper_ref, o_ref,
               kn_ref, acc_ref, run_ref, flag_ref, *, seq, tq, rows_per_pass):
    i = pl.program_id(2)
    half = _half_masks()
    scale = HEAD_DIM ** -0.5

    @pl.when(i == 0)
    def _():
        def fill(r, carry):
            rows = pl.ds(pl.multiple_of(r * rows_per_pass, rows_per_pass), rows_per_pass)
            kn_ref[rows, :] = _pair_rms(k_ref[rows, :].astype(F32), gk_ref[...], *half).astype(BF16)
            return carry
        lax.fori_loop(0, seq // rows_per_pass, fill, 0)

    qn = _pair_rms(q_ref[...].astype(F32), gq_ref[...], *half) * scale
    qh = [jnp.where(half[h], qn, 0.0).astype(BF16) for h in range(2)]
    acc_ref[...] = jnp.zeros_like(acc_ref)
    run_ref[...] = jnp.zeros_like(run_ref)

    def step(start, width, offset):
        rows = pl.ds(pl.multiple_of(start, tq), width)
        kk, vv = kn_ref[rows, :], v_ref[rows, :]
        up = upper_ref[:width, :width]
        z = [_dot_nt(qh[h], kk) for h in range(2)]
        log_beta = [_log_sigmoid(zh) for zh in z]
        log_rest = [lb - zh for lb, zh in zip(log_beta, z)]
        if offset is not None:
            row = lax.broadcasted_iota(jnp.int32, (tq, width), 0) + offset
            col = lax.broadcasted_iota(jnp.int32, (tq, width), 1)
            past = col < row
            log_rest = [jnp.where(past, lr, 0.0) for lr in log_rest]
        inner = [_dot(lr.astype(BF16), up) for lr in log_rest]
        w = [jnp.exp(log_beta[h] + inner[h] + run_ref[h]) for h in range(2)]
        if offset is not None:
            w = [jnp.where(past, wh, 0.0) for wh in w]
        for h in range(2):
            acc_ref[h] = acc_ref[h] + _dot(w[h].astype(BF16), vv)
        run = [run_ref[h] + jnp.sum(log_rest[h], axis=1, keepdims=True) for h in range(2)]
        for h in range(2):
            run_ref[h] = run[h]
        flag_ref[0] = jnp.maximum(jnp.max(run[0]), jnp.max(run[1]))

    first_block = jnp.maximum(i - 1, 0)
    step(first_block * tq, 2 * tq, (i - first_block) * tq)

    def cond(j):
        return jnp.logical_and(j >= 0, flag_ref[0] > -EXP_UNDERFLOW)

    def body(j):
        step(j * tq, tq, None)
        return j - 1

    lax.while_loop(cond, body, first_block - 1)
    o_ref[...] = jnp.where(half[0], acc_ref[0], acc_ref[1]).astype(o_ref.dtype)


def _sb_attention(big, gq2, gk2, batch, seq, tq=256):
    nq = seq // tq
    t = batch * seq
    upper = jnp.asarray(np.tril(np.ones((2 * tq, 2 * tq), np.float32), -1), BF16)
    kern = functools.partial(_sb_kernel, seq=seq, tq=tq, rows_per_pass=min(512, seq))
    return pl.pallas_call(
        kern,
        grid=(batch, N_HEADS // 2, nq),
        in_specs=[
            pl.BlockSpec((tq, LANES), lambda b, p, i: (b * nq + i, C_SB + p)),
            pl.BlockSpec((seq, LANES), lambda b, p, i: (b, C_SB + 4 + p)),
            pl.BlockSpec((seq, LANES), lambda b, p, i: (b, C_SB + 8 + p)),
            pl.BlockSpec((1, LANES), lambda b, p, i: (0, 0)),
            pl.BlockSpec((1, LANES), lambda b, p, i: (0, 0)),
            pl.BlockSpec((2 * tq, 2 * tq), lambda b, p, i: (0, 0)),
        ],
        out_specs=pl.BlockSpec((tq, LANES), lambda b, p, i: (b * nq + i, p)),
        out_shape=jax.ShapeDtypeStruct((t, BRANCH_WIDTH), BF16),
        scratch_shapes=[
            pltpu.VMEM((seq, LANES), BF16),
            pltpu.VMEM((2, tq, LANES), F32),
            pltpu.VMEM((2, tq, 1), F32),
            pltpu.SMEM((1,), F32),
        ],
        compiler_params=_params(("arbitrary", "arbitrary", "arbitrary")),
        name="sb_attention",
    )(big, big, big, gq2, gk2, upper)


def _gdn_prep_kernel(xq_ref, xk_ref, xv_ref, hq_ref, hk_ref, hv_ref, small_ref, cw_ref, alog_ref, dtb_ref,
                     grp_ref, eg_ref, eb_ref, tri_ref, q_ref, k_ref, v_ref, gc_ref, beta_ref, *, tiles_per_seq):
    first = (pl.program_id(0) % tiles_per_seq) == 0

    def conv_silu(x_ref, halo_ref, sec):
        halo = jnp.where(first, 0.0, halo_ref[...].astype(F32))
        xx = jnp.concatenate([halo, x_ref[...].astype(F32)], axis=0)
        hr = halo.shape[0]
        cols = slice(sec * BRANCH_WIDTH, (sec + 1) * BRANCH_WIDTH)
        y = cw_ref[GDN_CONV - 1:GDN_CONV, cols] * xx[hr:, :]
        for d in range(1, GDN_CONV):
            y = y + cw_ref[GDN_CONV - 1 - d:GDN_CONV - d, cols] * pltpu.roll(xx, d, 0)[hr:, :]
        return _silu(y)

    def l2(u):
        hi, lo = _split2(u * u)
        ss = _dot(hi, grp_ref[...]) + _dot(lo, grp_ref[...])
        return u * lax.rsqrt(ss + NORM_EPS)

    q_ref[...] = (l2(conv_silu(xq_ref, hq_ref, 0)) * (HEAD_DIM ** -0.5)).astype(q_ref.dtype)
    k_ref[...] = l2(conv_silu(xk_ref, hk_ref, 1)).astype(k_ref.dtype)
    v_ref[...] = conv_silu(xv_ref, hv_ref, 2).astype(v_ref.dtype)

    sm = small_ref[...]
    g = -jnp.exp(alog_ref[...]) * _softplus(sm + dtb_ref[...])
    g_wide = _dot_sel(g, eg_ref[...])
    gc_ref[...] = _sel_dot(tri_ref[...], g_wide)
    hi, lo = _split2(_sigmoid(sm))
    beta_ref[...] = _dot(hi, eb_ref[...]) + _dot(lo, eb_ref[...])


def _gdn_prep(big, small, conv_w, alog_row, dtb_row, batch, seq, tm=256):
    t = batch * seq
    halo_rows = 16
    grp = np.kron(np.eye(N_HEADS, dtype=np.float32), np.ones((HEAD_DIM, HEAD_DIM), np.float32))
    eg = np.zeros((LANES, BRANCH_WIDTH), np.float32)
    eb = np.zeros((LANES, BRANCH_WIDTH), np.float32)
    for h in range(N_HEADS):
        eg[L_GDNA + h, h * HEAD_DIM:(h + 1) * HEAD_DIM] = 1.0
        eb[L_GDNB + h, h * HEAD_DIM:(h + 1) * HEAD_DIM] = 1.0
    tri = np.kron(np.eye(tm // CHUNK, dtype=np.float32), np.tril(np.ones((CHUNK, CHUNK), np.float32)))
    wide = jax.ShapeDtypeStruct((t, BRANCH_WIDTH), F32)
    ratio = tm // halo_rows
    kern = functools.partial(_gdn_prep_kernel, tiles_per_seq=seq // tm)
    const = lambda shape: pl.BlockSpec(shape, lambda i: (0, 0))
    out_spec = pl.BlockSpec((tm, BRANCH_WIDTH), lambda i: (i, 0))
    sec0 = C_GDN * LANES // BRANCH_WIDTH
    x_specs = [pl.BlockSpec((tm, BRANCH_WIDTH), functools.partial(lambda i, s: (i, sec0 + s), s=s))
               for s in range(3)]
    halo_specs = [pl.BlockSpec((halo_rows, BRANCH_WIDTH),
                               functools.partial(lambda i, s: (jnp.maximum(i * ratio - 1, 0), sec0 + s), s=s))
                  for s in range(3)]
    return pl.pallas_call(
        kern,
        grid=(t // tm,),
        in_specs=x_specs + halo_specs + [
            pl.BlockSpec((tm, LANES), lambda i: (i, 0)),
            const((GDN_CONV, 3 * BRANCH_WIDTH)),
            const((1, LANES)),
            const((1, LANES)),
            const((BRANCH_WIDTH, BRANCH_WIDTH)),
            const((LANES, BRANCH_WIDTH)),
            const((LANES, BRANCH_WIDTH)),
            const((tm, tm)),
        ],
        out_specs=[out_spec] * 5,
        out_shape=[jax.ShapeDtypeStruct((t, BRANCH_WIDTH), BF16)] * 3 + [wide] * 2,
        compiler_params=_params(("arbitrary",)),
        name="gdn_prep",
    )(big, big, big, big, big, big, small, conv_w, alog_row, dtb_row, jnp.asarray(grp, BF16),
      jnp.asarray(eg, BF16), jnp.asarray(eb, BF16), jnp.asarray(tri, BF16))


def _stack_heads(x, half):
    return jnp.concatenate([jnp.where(half[0], x, 0.0), jnp.where(half[1], x, 0.0)], axis=0)


def _unstack(x):
    return x[:CHUNK, :] + x[CHUNK:, :]


def _gdn_kernel(q_ref, k_ref, v_ref, gc_ref, beta_ref, gain_ref, o_ref, state_ref, *, chunks):
    @pl.when(pl.program_id(1) == 0)
    def _():
        state_ref[...] = jnp.zeros_like(state_ref)

    half = _half_masks()
    n2 = 2 * CHUNK
    r = lax.broadcasted_iota(jnp.int32, (n2, n2), 0)
    c = lax.broadcasted_iota(jnp.int32, (n2, n2), 1)
    same = (r < CHUNK) == (c < CHUNK)
    lower = same & (c <= r)
    strict = same & (c < r)
    eye = (r == c).astype(F32)
    lane = _lane_iota((1, LANES))
    sub = lane % HEAD_DIM

    pairs = range(N_HEADS // 2)
    cps = [(ci, p) for ci in range(chunks) for p in pairs]
    win = {cp: (slice(cp[0] * CHUNK, (cp[0] + 1) * CHUNK), slice(cp[1] * LANES, (cp[1] + 1) * LANES))
           for cp in cps}

    diff_l, diff_r, kb_s, q_s, kk2, rhs, k_dec, q_dec, last = {}, {}, {}, {}, {}, {}, {}, {}, {}
    for cp in cps:
        rows, cols = win[cp]
        q2, k2, v2 = [ref[rows, cols].astype(F32) for ref in (q_ref, k_ref, v_ref)]
        gc2, beta2 = gc_ref[rows, cols], beta_ref[rows, cols]
        hi, lo, lo2 = [u.astype(F32) for u in _split3(gc2)]
        one = jnp.where((sub >= 3) & (sub < 6), 1.0, 0.0)
        la = jnp.where(sub == 0, hi, jnp.where(sub == 1, lo, jnp.where(sub == 2, lo2, one)))
        one = jnp.where(sub < 3, 1.0, 0.0)
        ra = jnp.where(sub == 3, -hi, jnp.where(sub == 4, -lo, jnp.where(sub == 5, -lo2, one))).astype(BF16)
        diff_l[cp] = _stack_heads(la, half).astype(BF16)
        diff_r[cp] = jnp.concatenate([ra, ra], axis=0)
        eg = jnp.exp(gc2)
        kb = k2 * beta2
        kk = k2.astype(BF16)
        kk2[cp] = jnp.concatenate([kk, kk], axis=0)
        kb_s[cp] = _stack_heads(kb, half).astype(BF16)
        q_s[cp] = _stack_heads(q2, half).astype(BF16)
        rhs[cp] = jnp.concatenate([_stack_heads(v2 * beta2, half), _stack_heads(kb * eg, half)],
                                  axis=1).astype(BF16)
        k_dec[cp] = (k2 * jnp.exp(gc2[CHUNK - 1:CHUNK, :] - gc2)).astype(BF16)
        q_dec[cp] = q2 * eg
        last[cp] = eg[CHUNK - 1:CHUNK, :]

    diff = {cp: _dot_nt(diff_l[cp], diff_r[cp]) for cp in cps}
    akk = {cp: _dot_nt(kb_s[cp], kk2[cp]) for cp in cps}
    aqk = {cp: _dot_nt(q_s[cp], kk2[cp]) for cp in cps}

    x, pw, intra = {}, {}, {}
    for cp in cps:
        decay = jnp.where(lower, jnp.exp(jnp.minimum(diff[cp], 0.0)), 0.0)
        a = jnp.where(strict, akk[cp] * decay, 0.0)
        intra[cp] = (aqk[cp] * decay).astype(BF16)
        x[cp] = eye - a
        pw[cp] = a

    for _ in range(5):
        for cp in cps:
            pb = pw[cp].astype(BF16)
            pw[cp] = _dot(pb, pb)
        for cp in cps:
            x[cp] = x[cp] + _dot(x[cp].astype(BF16), pw[cp].astype(BF16))

    uw = {cp: _dot(x[cp].astype(BF16), rhs[cp]) for cp in cps}
    u2 = {cp: _unstack(uw[cp][:, :LANES]) for cp in cps}
    w2 = {cp: _unstack(uw[cp][:, LANES:]) for cp in cps}

    uw_stack = {cp: jnp.concatenate([_stack_heads(u2[cp], half), _stack_heads(w2[cp], half)],
                                    axis=1).astype(BF16) for cp in cps}
    uw_flat = {cp: jnp.concatenate([u2[cp], w2[cp]], axis=1).astype(BF16) for cp in cps}
    iuw = {cp: _dot(intra[cp], uw_stack[cp]) for cp in cps}
    kuw = {cp: _dot_tn(k_dec[cp], uw_flat[cp]) for cp in cps}
    o_loc, lhs, s_add = {}, {}, {}
    for cp in cps:
        o_loc[cp] = _unstack(iuw[cp][:, :LANES])
        q_eff = q_dec[cp] - _unstack(iuw[cp][:, LANES:])
        m_eff = jnp.where(same, -kuw[cp][:, LANES:], 0.0)
        lhs[cp] = jnp.concatenate([q_eff, m_eff], axis=0).astype(BF16)
        s_add[cp] = jnp.where(same, kuw[cp][:, :LANES], 0.0)

    state = {p: state_ref[p] for p in pairs}
    for ci in range(chunks):
        res = {}
        for p in pairs:
            hi, lo = _split2(state[p])
            res[p] = _dot(lhs[(ci, p)], jnp.concatenate([hi, lo], axis=1))
        for p in pairs:
            cp = (ci, p)
            rr = res[p][:, :LANES] + res[p][:, LANES:]
            o2 = rr[:CHUNK, :] + o_loc[cp]
            state[p] = state[p] * last[cp] + rr[CHUNK:, :] + s_add[cp]
            rows, cols = win[cp]
            o_ref[rows, cols] = _pair_rms(o2, gain_ref[...], *half).astype(o_ref.dtype)
    for p in pairs:
        state_ref[p] = state[p]


def _gdn_scan(qn, kn, vv, gc, beta, gain2, batch, seq, chunks=4):
    t = batch * seq
    tm = chunks * CHUNK
    nb = seq // tm
    spec = pl.BlockSpec((tm, BRANCH_WIDTH), lambda b, n: (b * nb + n, 0))
    return pl.pallas_call(
        functools.partial(_gdn_kernel, chunks=chunks),
        grid=(batch, nb),
        in_specs=[spec] * 5 + [pl.BlockSpec((1, LANES), lambda b, n: (0, 0))],
        out_specs=spec,
        out_shape=jax.ShapeDtypeStruct((t, BRANCH_WIDTH), BF16),
        scratch_shapes=[pltpu.VMEM((N_HEADS // 2, LANES, LANES), F32)],
        compiler_params=_params(("arbitrary", "arbitrary")),
        name="gdn_scan",
    )(qn, kn, vv, gc, beta, gain2)


GLA_LEVELS = 6


def _gla_constants():
    n = CHUNK
    t = np.arange(n)
    blocks = []
    masks = []
    for lev in range(GLA_LEVELS):
        w = 1 << lev
        lower = (t // w) % 2 == 1
        mid_q = (t // w) * w
        cq = (lower[:, None] & (t[None, :] > mid_q[:, None]) & (t[None, :] <= t[:, None])).astype(np.float32)
        mid_k = (t // w + 1) * w
        ck = ((~lower)[:, None] & (t[None, :] > t[:, None]) & (t[None, :] <= mid_k[:, None])).astype(np.float32)
        blocks += [cq, ck]
        m = lower[:, None] & (~lower)[None, :] & ((t // (2 * w))[:, None] == (t // (2 * w))[None, :])
        masks.append(np.kron(np.eye(2), m.astype(np.float32)))
    masks.append(np.eye(2 * n, dtype=np.float32))
    tri = np.tril(np.ones((n, n), np.float32))
    rev = (t[None, :] > t[:, None]).astype(np.float32)
    blocks += [tri, rev]
    return np.concatenate(blocks, axis=0), np.stack(masks)


def _gla_kernel(x_ref, small_ref, w2_ref, bias_ref, sel_ref, mask_ref, gain_ref, o_ref, state_ref, *, chunks):
    @pl.when(pl.program_id(1) == 0)
    def _():
        state_ref[...] = jnp.zeros_like(state_ref)

    half = _half_masks()
    qk_w = GLA_HEADS * HEAD_DIM
    r = lax.broadcasted_iota(jnp.int32, (2 * LANES, LANES), 0)
    c = lax.broadcasted_iota(jnp.int32, (2 * LANES, LANES), 1)
    state_mask = (r < LANES) == (c < HEAD_DIM)
    zeros_v = jnp.zeros((CHUNK, LANES), F32)

    pairs = range(GLA_HEADS // 2)
    cps = [(ci, p) for ci in range(chunks) for p in pairs]
    crow = lambda ci: slice(ci * CHUNK, (ci + 1) * CHUNK)

    log_a = {}
    for ci in range(chunks):
        y = _dot(small_ref[crow(ci), :].astype(BF16), w2_ref[...]) + bias_ref[...]
        log_a[ci] = _log_sigmoid(y) * (1.0 / GLA_GATE_TAU)

    q2, k2, v_pair, la_split = {}, {}, {}, {}
    for ci, p in cps:
        rows = crow(ci)
        q2[ci, p] = x_ref[rows, p * LANES:(p + 1) * LANES].astype(F32) * (HEAD_DIM ** -0.5)
        k2[ci, p] = x_ref[rows, qk_w + p * LANES:qk_w + (p + 1) * LANES].astype(F32)
        v_pair[ci, p] = x_ref[rows, 2 * qk_w + 2 * p * LANES:2 * qk_w + 2 * (p + 1) * LANES]
        hi, lo = _split2(log_a[ci][:, p * LANES:(p + 1) * LANES])
        la_split[ci, p] = jnp.concatenate([hi, lo], axis=1)

    sums = {}
    for cp in cps:
        s2 = _dot(sel_ref[...], la_split[cp])
        sums[cp] = s2[:, :LANES] + s2[:, LANES:]

    a = {cp: None for cp in cps}
    for lev in range(GLA_LEVELS + 1):
        lhs, rhs = {}, {}
        for cp in cps:
            if lev < GLA_LEVELS:
                dq = sums[cp][(2 * lev) * CHUNK:(2 * lev + 1) * CHUNK, :]
                dk = sums[cp][(2 * lev + 1) * CHUNK:(2 * lev + 2) * CHUNK, :]
                ql, kl = q2[cp] * jnp.exp(dq), k2[cp] * jnp.exp(dk)
            else:
                ql, kl = q2[cp], k2[cp]
            kb = kl.astype(BF16)
            lhs[cp] = _stack_heads(ql, half).astype(BF16)
            rhs[cp] = jnp.concatenate([kb, kb], axis=0)
        prod = {cp: _dot_nt(lhs[cp], rhs[cp]) for cp in cps}
        for cp in cps:
            term = prod[cp] * mask_ref[lev]
            a[cp] = term if a[cp] is None else a[cp] + term

    v_stack, q_dec, k_dec, b_last = {}, {}, {}, {}
    for cp in cps:
        b = sums[cp][2 * GLA_LEVELS * CHUNK:(2 * GLA_LEVELS + 1) * CHUNK, :]
        b_rev = sums[cp][(2 * GLA_LEVELS + 1) * CHUNK:, :]
        vf = v_pair[cp].astype(F32)
        v_stack[cp] = jnp.concatenate(
            [jnp.concatenate([vf[:, :LANES], zeros_v], axis=1),
             jnp.concatenate([zeros_v, vf[:, LANES:]], axis=1)], axis=0).astype(BF16)
        q_dec[cp] = (q2[cp] * jnp.exp(b)).astype(BF16)
        k_dec[cp] = (k2[cp] * jnp.exp(b_rev)).astype(BF16)
        b_last[cp] = jnp.exp(b[CHUNK - 1:CHUNK, :])
    o_loc = {cp: _unstack(_dot(a[cp].astype(BF16), v_stack[cp])) for cp in cps}
    upd = {cp: jnp.where(state_mask, _dot_tn(v_pair[cp], k_dec[cp]), 0.0) for cp in cps}

    state = {p: state_ref[p] for p in pairs}
    for ci in range(chunks):
        o_pair = {p: o_loc[ci, p] + _dot_nt(q_dec[ci, p], state[p].astype(BF16)) for p in pairs}
        for p in pairs:
            state[p] = state[p] * b_last[ci, p] + upd[ci, p]
            for h in range(2):
                oh = o_pair[p][:, h * LANES:(h + 1) * LANES]
                ms = jnp.mean(oh * oh, axis=-1, keepdims=True)
                col0 = (2 * p + h) * LANES
                o_ref[crow(ci), col0:col0 + LANES] = (
                    oh * lax.rsqrt(ms + NORM_EPS) * gain_ref[...]).astype(o_ref.dtype)
    for p in pairs:
        state_ref[p] = state[p]


def _gla(big, small, w2_ext, bias_row, gain_row, batch, seq, chunks=4):
    t = batch * seq
    tm = chunks * CHUNK
    nb = seq // tm
    sel, masks = _gla_constants()
    width = 2 * GLA_HEADS * HEAD_DIM + BRANCH_WIDTH
    const2 = lambda shape: pl.BlockSpec(shape, lambda b, n: (0, 0))
    return pl.pallas_call(
        functools.partial(_gla_kernel, chunks=chunks),
        grid=(batch, nb),
        in_specs=[
            pl.BlockSpec((tm, width), lambda b, n: (b * nb + n, C_GLA * LANES // width)),
            pl.BlockSpec((tm, LANES), lambda b, n: (b * nb + n, 0)),
            const2((LANES, GLA_HEADS * HEAD_DIM)),
            const2((1, GLA_HEADS * HEAD_DIM)),
            const2(sel.shape),
            pl.BlockSpec(masks.shape, lambda b, n: (0, 0, 0)),
            const2((1, LANES)),
        ],
        out_specs=pl.BlockSpec((tm, BRANCH_WIDTH), lambda b, n: (b * nb + n, 0)),
        out_shape=jax.ShapeDtypeStruct((t, BRANCH_WIDTH), BF16),
        scratch_shapes=[pltpu.VMEM((GLA_HEADS // 2, 2 * LANES, LANES), F32)],
        compiler_params=_params(("arbitrary", "arbitrary")),
        name="gla",
    )(big, small, w2_ext, bias_row, jnp.asarray(sel, BF16), jnp.asarray(masks, F32), gain_row)


def _merge_kernel(x_ref, o0_ref, o1_ref, o2_ref, o3_ref, z_ref, ml_ref, gb_ref, wb_ref, wo_ref, out_ref):
    merged = None
    for n, o_ref in enumerate((o0_ref, o1_ref, o2_ref, o3_ref)):
        z = z_ref[:, n * BRANCH_WIDTH:(n + 1) * BRANCH_WIDTH].astype(F32)
        br = (o_ref[...].astype(F32) * _silu(z)).astype(BF16)
        proj = _dot(br, wb_ref[n])
        gate = _sigmoid(ml_ref[:, n * D_MODEL:(n + 1) * D_MODEL].astype(F32)
                        + gb_ref[:, n * D_MODEL:(n + 1) * D_MODEL])
        term = proj * gate
        merged = term if merged is None else merged + term
    out_ref[...] = x_ref[...] + _dot(merged.astype(BF16), wo_ref[...])


def _merge(x2, outs, big, gate_bias_row, w_branch, w_out, tm=256):
    t = x2.shape[0]
    zw = N_BRANCH * BRANCH_WIDTH
    mw = N_BRANCH * D_MODEL
    o_spec = pl.BlockSpec((tm, BRANCH_WIDTH), lambda i: (i, 0))
    return pl.pallas_call(
        _merge_kernel,
        grid=(t // tm,),
        in_specs=[
            pl.BlockSpec((tm, D_MODEL), lambda i: (i, 0)),
            o_spec, o_spec, o_spec, o_spec,
            pl.BlockSpec((tm, zw), lambda i: (i, C_Z * LANES // zw)),
            pl.BlockSpec((tm, mw), lambda i: (i, C_MERGE * LANES // mw)),
            pl.BlockSpec((1, mw), lambda i: (0, 0)),
            pl.BlockSpec((N_BRANCH, BRANCH_WIDTH, D_MODEL), lambda i: (0, 0, 0), pipeline_mode=pl.Buffered(1)),
            pl.BlockSpec((D_MODEL, D_MODEL), lambda i: (0, 0), pipeline_mode=pl.Buffered(1)),
        ],
        out_specs=pl.BlockSpec((tm, D_MODEL), lambda i: (i, 0)),
        out_shape=jax.ShapeDtypeStruct((t, D_MODEL), F32),
        compiler_params=_params(("arbitrary",)),
        name="merge_out",
    )(x2, *outs, big, big, gate_bias_row, w_branch, w_out)


def _reorder_w_in(w_in):
    sizes = (3 * BRANCH_WIDTH, N_HEADS, BRANCH_WIDTH, 3 * BRANCH_WIDTH, BRANCH_WIDTH,
             3 * BRANCH_WIDTH, N_HEADS, N_HEADS, BRANCH_WIDTH,
             2 * GLA_HEADS * HEAD_DIM, BRANCH_WIDTH, GLA_GATE_RANK, BRANCH_WIDTH, N_BRANCH * D_MODEL)
    (fox_qkv, fox_f, fox_z, sb_qkv, sb_z, gdn_qkv, gdn_a, gdn_b, gdn_z,
     gla_qk, gla_v, gla_g, gla_z, merge) = jnp.split(w_in, np.cumsum(sizes)[:-1].tolist(), axis=1)
    big = jnp.concatenate([merge, fox_z, sb_z, gdn_z, gla_z, gla_qk, gla_v, gdn_qkv, fox_qkv, sb_qkv], axis=1)
    pad = jnp.zeros((w_in.shape[0], LANES - (3 * N_HEADS + GLA_GATE_RANK)), w_in.dtype)
    small = jnp.concatenate([fox_f, gdn_a, gdn_b, gla_g, pad], axis=1)
    return big.astype(BF16), small.astype(BF16)


def _lane_row(values, offset):
    row = jnp.zeros((1, LANES), F32)
    return row.at[0, offset:offset + values.shape[0]].set(values.astype(F32))


def _layer(x2, batch, seq, norm_g, w_in, fox_f_bias, fox_q_norm, fox_k_norm, sb_q_norm, sb_k_norm,
           gdn_conv_w, gdn_a_log, gdn_dt_bias, gdn_o_norm, gla_gate_w2, gla_gate_bias, gla_o_norm,
           gate_bias, w_branch, w_out):
    w_big, w_small = _reorder_w_in(w_in)
    g_row = norm_g.reshape(1, D_MODEL).astype(F32)
    t = batch * seq
    big = _norm_proj(x2, g_row, w_big, BF16, min(2048, t), 512)
    small = _norm_proj(x2, g_row, w_small, F32, min(1024, t), LANES)

    pair = lambda g: jnp.tile(g.astype(F32), 2).reshape(1, LANES)

    c = _fox_cum(small, _lane_row(fox_f_bias, L_FOXF), batch, seq)
    tq = tk = min(512, seq)
    c3 = c.reshape(batch, seq, LANES)[:, :, :N_HEADS]
    first = jnp.transpose(c3[:, 0::tq, :], (0, 2, 1))
    last = jnp.transpose(c3[:, tk - 1::tk, :], (0, 2, 1))
    qk_bound = 1.02 * HEAD_DIM ** 0.5 * jnp.max(jnp.abs(fox_q_norm)) * jnp.max(jnp.abs(fox_k_norm))
    thr = EXP_UNDERFLOW + 2.0 * qk_bound + 1.0
    tab = jnp.concatenate([first.reshape(-1), last.reshape(-1), thr.reshape(1), qk_bound.reshape(1)]).astype(F32)
    o_fox = _fox_attention(big, c, tab, pair(fox_q_norm), pair(fox_k_norm), batch, seq, tq, tk)

    o_sb = _sb_attention(big, pair(sb_q_norm), pair(sb_k_norm), batch, seq, min(256, seq))

    qn, kn, vv, gc, beta = _gdn_prep(big, small, gdn_conv_w.astype(F32), _lane_row(gdn_a_log, L_GDNA),
                                     _lane_row(gdn_dt_bias, L_GDNA), batch, seq, min(256, seq))
    o_gdn = _gdn_scan(qn, kn, vv, gc, beta, pair(gdn_o_norm), batch, seq)

    w2_ext = jnp.zeros((LANES, GLA_HEADS * HEAD_DIM), F32).at[L_GLAG:L_GLAG + GLA_GATE_RANK].set(
        gla_gate_w2.astype(F32)).astype(BF16)
    o_gla = _gla(big, small, w2_ext, gla_gate_bias.reshape(1, -1).astype(F32),
                 gla_o_norm.reshape(1, LANES).astype(F32), batch, seq)

    return _merge(x2, (o_fox, o_sb, o_gdn, o_gla), big, gate_bias.reshape(1, -1).astype(F32),
                  w_branch.astype(BF16), w_out.astype(BF16), min(512, t))


def kernel(x, norm_g, w_in, fox_f_bias, fox_q_norm, fox_k_norm, sb_q_norm, sb_k_norm, gdn_conv_w,
           gdn_a_log, gdn_dt_bias, gdn_o_norm, gla_gate_w2, gla_gate_bias, gla_o_norm, gate_bias,
           w_branch, w_out):
    batch, seq, d = x.shape
    x2 = x.reshape(batch * seq, d)
    params = (norm_g, w_in, fox_f_bias, fox_q_norm, fox_k_norm, sb_q_norm, sb_k_norm, gdn_conv_w,
              gdn_a_log, gdn_dt_bias, gdn_o_norm, gla_gate_w2, gla_gate_bias, gla_o_norm, gate_bias,
              w_branch, w_out)
    for l in range(norm_g.shape[0]):
        x2 = _layer(x2, batch, seq, *[p[l] for p in params])
    return x2.reshape(batch, seq, d)
```

```python
import functools

import numpy as np
import jax
import jax.numpy as jnp
from jax import lax
from jax.experimental import pallas as pl
from jax.experimental.pallas import tpu as pltpu

F32 = jnp.float32
BF16 = jnp.bfloat16

LANES = 128
HEAD_DIM = 64
D_MODEL = 1024
BRANCH_WIDTH = 512
N_BRANCH = 4
N_HEADS = 8
GLA_HEADS = 4
GLA_GATE_RANK = 16
GLA_GATE_TAU = 16.0
GDN_CONV = 4
CHUNK = 64
NORM_EPS = 1e-6
VMEM_LIMIT = 48 * 1024 * 1024
EXP_UNDERFLOW = 104.0
NEG_BIG = -1e30
NEG_LOG2E = -1.4426950408889634
STATIC_LOGIT_LIMIT = 20.0

C_MERGE, C_Z, C_GLA, C_GDN, C_FOX, C_SB, C_END = 0, 32, 48, 56, 68, 80, 92
N_BIG = C_END * LANES
L_FOXF, L_GDNA, L_GDNB, L_GLAG = 0, 8, 16, 24


def _dot(a, b):
    return jnp.dot(a, b, preferred_element_type=F32)


def _dot_nt(a, b):
    return lax.dot_general(a, b, (((1,), (1,)), ((), ())), preferred_element_type=F32)


def _dot_tn(a, b):
    return lax.dot_general(a, b, (((0,), (0,)), ((), ())), preferred_element_type=F32)


def _split2(x):
    hi = x.astype(BF16)
    lo = (x - hi.astype(F32)).astype(BF16)
    return hi, lo


def _split3(x):
    hi = x.astype(BF16)
    r = x - hi.astype(F32)
    lo = r.astype(BF16)
    lo2 = (r - lo.astype(F32)).astype(BF16)
    return hi, lo, lo2


def _sel_dot(sel, x):
    hi, lo, lo2 = _split3(x)
    return _dot(sel, hi) + _dot(sel, lo) + _dot(sel, lo2)


def _dot_sel(x, sel):
    hi, lo, lo2 = _split3(x)
    return _dot(hi, sel) + _dot(lo, sel) + _dot(lo2, sel)


def _softplus(x):
    return jnp.maximum(x, 0.0) + jnp.log(1.0 + jnp.exp(-jnp.abs(x)))


def _log_sigmoid(x):
    return jnp.minimum(x, 0.0) - jnp.log(1.0 + jnp.exp(-jnp.abs(x)))


def _sigmoid(x):
    return 1.0 / (1.0 + jnp.exp(-x))


def _silu(x):
    return x / (1.0 + jnp.exp(-x))


def _lane_iota(shape):
    return lax.broadcasted_iota(jnp.int32, shape, len(shape) - 1)


def _half_masks():
    lane = _lane_iota((1, LANES))
    return lane < HEAD_DIM, lane >= HEAD_DIM


def _params(semantics):
    return pltpu.CompilerParams(dimension_semantics=semantics, vmem_limit_bytes=VMEM_LIMIT)


def _proj_kernel(x_ref, g_ref, w_ref, ws_ref, o_ref, os_ref, h_ref):
    @pl.when(pl.program_id(1) == 0)
    def _():
        x = x_ref[...]
        ms = jnp.mean(x * x, axis=-1, keepdims=True)
        h_ref[...] = (x * lax.rsqrt(ms + NORM_EPS) * g_ref[...]).astype(BF16)
        os_ref[...] = _dot(h_ref[...], ws_ref[...])

    o_ref[...] = _dot(h_ref[...], w_ref[...]).astype(o_ref.dtype)


def _norm_proj(x2, g, w, w_small, tm, tn):
    t, d = x2.shape
    n = w.shape[1]
    return pl.pallas_call(
        _proj_kernel,
        grid=(t // tm, n // tn),
        in_specs=[
            pl.BlockSpec((tm, d), lambda i, j: (i, 0)),
            pl.BlockSpec((1, d), lambda i, j: (0, 0)),
            pl.BlockSpec((d, tn), lambda i, j: (0, j)),
            pl.BlockSpec((d, LANES), lambda i, j: (0, 0)),
        ],
        out_specs=[pl.BlockSpec((tm, tn), lambda i, j: (i, j)),
                   pl.BlockSpec((tm, LANES), lambda i, j: (i, 0))],
        out_shape=[jax.ShapeDtypeStruct((t, n), BF16), jax.ShapeDtypeStruct((t, LANES), F32)],
        scratch_shapes=[pltpu.VMEM((tm, d), BF16)],
        compiler_params=_params(("arbitrary", "arbitrary")),
        name="norm_proj",
    )(x2, g, w, w_small)


def _fox_cum_kernel(f_ref, bias_ref, tri_ref, c_ref, carry_ref):
    @pl.when(pl.program_id(1) == 0)
    def _():
        carry_ref[...] = jnp.zeros_like(carry_ref)

    lf = _log_sigmoid(f_ref[...] + bias_ref[...])
    cs = _sel_dot(tri_ref[...], lf) + carry_ref[...]
    c_ref[...] = cs
    carry_ref[...] = cs[-1:, :]


def _fox_cum(small, bias_row, batch, seq, tr=512):
    tri = jnp.asarray(np.tril(np.ones((tr, tr), np.float32)), BF16)
    nb = seq // tr
    return pl.pallas_call(
        _fox_cum_kernel,
        grid=(batch, nb),
        in_specs=[
            pl.BlockSpec((tr, LANES), lambda b, r: (b * nb + r, 0)),
            pl.BlockSpec((1, LANES), lambda b, r: (0, 0)),
            pl.BlockSpec((tr, tr), lambda b, r: (0, 0)),
        ],
        out_specs=pl.BlockSpec((tr, LANES), lambda b, r: (b * nb + r, 0)),
        out_shape=jax.ShapeDtypeStruct(small.shape, F32),
        scratch_shapes=[pltpu.VMEM((1, LANES), F32)],
        compiler_params=_params(("arbitrary", "arbitrary")),
        name="fox_cum",
    )(small, bias_row, tri)


def _pair_rms(x, gain, half0, half1):
    sq = x * x
    ms0 = jnp.sum(jnp.where(half0, sq, 0.0), axis=-1, keepdims=True) * (1.0 / HEAD_DIM)
    ms1 = jnp.sum(jnp.where(half1, sq, 0.0), axis=-1, keepdims=True) * (1.0 / HEAD_DIM)
    r = jnp.where(half0, lax.rsqrt(ms0 + NORM_EPS), lax.rsqrt(ms1 + NORM_EPS))
    return x * r * gain


def _lane_pick(x, idx):
    lane = _lane_iota((1, LANES))
    return jnp.sum(jnp.where(lane == idx, x, 0.0), axis=-1, keepdims=True)


def _aug_lanes(c_col, base, key_side):
    lane = _lane_iota((1, LANES))
    hi, lo, lo2 = [v.astype(F32) for v in _split3(c_col)]
    if key_side:
        hi, lo, lo2 = -hi, -lo, -lo2
        vals, ones = (base + 3, base + 4, base + 5), (base, base + 1, base + 2)
    else:
        vals, ones = (base, base + 1, base + 2), (base + 3, base + 4, base + 5)
    out = jnp.where((lane == ones[0]) | (lane == ones[1]) | (lane == ones[2]), 1.0, 0.0)
    out = jnp.where(lane == vals[0], hi, out)
    out = jnp.where(lane == vals[1], lo, out)
    out = jnp.where(lane == vals[2], lo2, out)
    return out


def _fox_kernel(tab_ref, q_ref, k_ref, v_ref, cq_ref, ck_ref, gq_ref, gk_ref, o_ref,
                kaug_ref, vaug_ref, acc_ref, m_ref, l_ref, *, seq, tq, tk, rows_per_pass):
    b, p, i = pl.program_id(0), pl.program_id(1), pl.program_id(2)
    nq, nk = seq // tq, seq // tk
    nb = pl.num_programs(0)
    half = _half_masks()
    lane = _lane_iota((1, LANES))
    scale = HEAD_DIM ** -0.5
    spare = [HEAD_DIM * (1 - h) for h in range(2)]

    @pl.when(i == 0)
    def _():
        def fill(r, carry):
            rows = pl.ds(pl.multiple_of(r * rows_per_pass, rows_per_pass), rows_per_pass)
            kn = _pair_rms(k_ref[rows, :].astype(F32), gk_ref[...], *half)
            cc = ck_ref[rows, :]
            vv = v_ref[rows, :]
            for h in range(2):
                aug = _aug_lanes(_lane_pick(cc, 2 * p + h), spare[h], True)
                kaug_ref[h, rows, :] = jnp.where(half[h], kn, aug).astype(BF16)
                ones = jnp.where(lane == spare[h], 1.0, 0.0).astype(BF16)
                vaug_ref[h, rows, :] = jnp.where(half[h], vv, ones)
            return carry
        lax.fori_loop(0, seq // rows_per_pass, fill, 0)

    first_off = N_HEADS * nb * nq
    thr = tab_ref[first_off + N_HEADS * nb * nk]
    bound = tab_ref[first_off + N_HEADS * nb * nk + 1]
    fast = bound < STATIC_LOGIT_LIMIT

    qn = _pair_rms(q_ref[...].astype(F32), gq_ref[...], *half) * scale
    cq = cq_ref[...]
    tile_i = i // (tk // tq)
    row = lax.broadcasted_iota(jnp.int32, (tq, tk), 0) + (i % (tk // tq)) * tq
    col = lax.broadcasted_iota(jnp.int32, (tq, tk), 1)
    causal = col <= row

    c_first = [tab_ref[(b * N_HEADS + 2 * p + h) * nq + i] for h in range(2)]

    def cond(j):
        jj = jnp.maximum(j, 0)
        need = [c_first[h] - tab_ref[first_off + (b * N_HEADS + 2 * p + h) * nk + jj] >= -thr for h in range(2)]
        return jnp.logical_and(j >= 0, jnp.logical_or(need[0], need[1]))

    def walk(tile):
        tile(tile_i, True)

        def body(j):
            tile(j, False)
            return j - 1
        lax.while_loop(cond, body, tile_i - 1)

    def q_aug(shift):
        return [jnp.where(half[h], qn, _aug_lanes(_lane_pick(cq, 2 * p + h) - shift, spare[h], False)).astype(BF16)
                for h in range(2)]

    @pl.when(fast)
    def _():
        qa = q_aug(bound)
        acc_ref[...] = jnp.zeros_like(acc_ref)

        def tile(j, masked):
            rows = pl.ds(pl.multiple_of(j * tk, tk), tk)
            s = [_dot_nt(qa[h], kaug_ref[h, rows, :]) for h in range(2)]
            if masked:
                s = [jnp.where(causal, sh, NEG_BIG) for sh in s]
            pr = [jnp.exp(sh).astype(BF16) for sh in s]
            for h in range(2):
                acc_ref[h] = acc_ref[h] + _dot(pr[h], vaug_ref[h, rows, :])

        walk(tile)
        out = [acc_ref[h] / acc_ref[h][:, spare[h]:spare[h] + 1] for h in range(2)]
        o_ref[...] = jnp.where(half[0], out[0], out[1]).astype(o_ref.dtype)

    @pl.when(jnp.logical_not(fast))
    def _():
        qa = q_aug(0.0)
        acc_ref[...] = jnp.zeros_like(acc_ref)
        m_ref[...] = jnp.full_like(m_ref, NEG_BIG)
        l_ref[...] = jnp.zeros_like(l_ref)

        def tile(j, masked):
            rows = pl.ds(pl.multiple_of(j * tk, tk), tk)
            for h in range(2):
                s = _dot_nt(qa[h], kaug_ref[h, rows, :])
                if masked:
                    s = jnp.where(causal, s, NEG_BIG)
                m_prev = m_ref[h]
                m_new = jnp.maximum(m_prev, jnp.max(s, axis=1, keepdims=True))
                alpha = jnp.exp(m_prev - m_new)
                pr = jnp.exp(s - m_new)
                l_ref[h] = alpha * l_ref[h] + jnp.sum(pr, axis=1, keepdims=True)
                acc_ref[h] = alpha * acc_ref[h] + _dot(pr.astype(BF16), vaug_ref[h, rows, :])
                m_ref[h] = m_new

        walk(tile)
        out = [acc_ref[h] / l_ref[h] for h in range(2)]
        o_ref[...] = jnp.where(half[0], out[0], out[1]).astype(o_ref.dtype)


def _fox_attention(big, c, tab, gq2, gk2, batch, seq, tq, tk):
    nq = seq // tq
    t = batch * seq
    kern = functools.partial(_fox_kernel, seq=seq, tq=tq, tk=tk, rows_per_pass=min(512, seq))
    grid_spec = pltpu.PrefetchScalarGridSpec(
        num_scalar_prefetch=1,
        grid=(batch, N_HEADS // 2, nq),
        in_specs=[
            pl.BlockSpec((tq, LANES), lambda b, p, i, tab: (b * nq + i, C_FOX + p)),
            pl.BlockSpec((seq, LANES), lambda b, p, i, tab: (b, C_FOX + 4 + p)),
            pl.BlockSpec((seq, LANES), lambda b, p, i, tab: (b, C_FOX + 8 + p)),
            pl.BlockSpec((tq, LANES), lambda b, p, i, tab: (b * nq + i, 0)),
            pl.BlockSpec((seq, LANES), lambda b, p, i, tab: (b, 0)),
            pl.BlockSpec((1, LANES), lambda b, p, i, tab: (0, 0)),
            pl.BlockSpec((1, LANES), lambda b, p, i, tab: (0, 0)),
        ],
        out_specs=pl.BlockSpec((tq, LANES), lambda b, p, i, tab: (b * nq + i, p)),
        scratch_shapes=[
            pltpu.VMEM((2, seq, LANES), BF16),
            pltpu.VMEM((2, seq, LANES), BF16),
            pltpu.VMEM((2, tq, LANES), F32),
            pltpu.VMEM((2, tq, 1), F32),
            pltpu.VMEM((2, tq, 1), F32),
        ],
    )
    return pl.pallas_call(
        kern,
        grid_spec=grid_spec,
        out_shape=jax.ShapeDtypeStruct((t, BRANCH_WIDTH), BF16),
        compiler_params=_params(("arbitrary", "arbitrary", "arbitrary")),
        name="fox_attention",
    )(tab, big, big, big, c, c, gq2, gk2)


SB_MASKED_LOGIT = -1e4


def _sb_kernel(bound_ref, q_ref, k_ref, v_ref, gq_ref, gk_ref, lower_ref, o_ref,
               kn_ref, acc_ref, run_ref, flag_ref, *, seq, tq, nsub, rows_per_pass):
    i = pl.program_id(2)
    half = _half_masks()
    scale = HEAD_DIM ** -0.5

    @pl.when(i == 0)
    def _():
        def fill(r, carry):
            rows = pl.ds(pl.multiple_of(r * rows_per_pass, rows_per_pass), rows_per_pass)
            kn_ref[rows, :] = _pair_rms(k_ref[rows, :].astype(F32), gk_ref[...], *half).astype(BF16)
            return carry
        lax.fori_loop(0, seq // rows_per_pass, fill, 0)

    qn = _pair_rms(q_ref[...].astype(F32), gq_ref[...], *half) * scale
    qh = [[jnp.where(half[h], qn[s * tq:(s + 1) * tq], 0.0).astype(BF16) for h in range(2)] for s in range(nsub)]
    acc_ref[...] = jnp.zeros_like(acc_ref)
    run_ref[...] = jnp.zeros_like(run_ref)
    block = [i * nsub + s for s in range(nsub)]
    first_block = [jnp.maximum(blk - 1, 0) for blk in block]

    def step(tiles, bounded):
        kk, vv, low, z = {}, {}, {}, {}
        for s, start, width, offset in tiles:
            rows = pl.ds(pl.multiple_of(start, tq), width)
            kk[s], vv[s], low[s] = kn_ref[rows, :], v_ref[rows, :], lower_ref[:width, :width]
        for s, start, width, offset in tiles:
            for h in range(2):
                z[s, h] = _dot_nt(qh[s][h], kk[s])
        for s, start, width, offset in tiles:
            if offset is not None:
                row = lax.broadcasted_iota(jnp.int32, (tq, width), 0) + offset
                col = lax.broadcasted_iota(jnp.int32, (tq, width), 1)
                for h in range(2):
                    z[s, h] = jnp.where(col < row, z[s, h], SB_MASKED_LOGIT)
        keys = list(z)
        if bounded:
            log_rest = {k: -jnp.log(1.0 + jnp.exp(z[k])) for k in keys}
        else:
            log_rest = {k: -_softplus(z[k]) for k in keys}
        inner = {k: _dot(log_rest[k].astype(BF16), low[k[0]]) for k in keys}
        w = {k: jnp.exp((z[k] + run_ref[k[0], k[1]]) + inner[k]).astype(BF16) for k in keys}
        for k in keys:
            acc_ref[k[0], k[1]] = acc_ref[k[0], k[1]] + _dot(w[k], vv[k[0]])
        run = {k: run_ref[k[0], k[1]] + jnp.sum(log_rest[k], axis=1, keepdims=True) for k in keys}
        for k in keys:
            run_ref[k[0], k[1]] = run[k]
        for s, start, width, offset in tiles:
            flag_ref[s] = jnp.maximum(jnp.max(run[s, 0]), jnp.max(run[s, 1]))

    def walk(bounded):
        step([(s, first_block[s] * tq, 2 * tq, (block[s] - first_block[s]) * tq) for s in range(nsub)], bounded)
        for s in range(nsub):
            def cond(j, s=s):
                return jnp.logical_and(j >= 0, flag_ref[s] > -EXP_UNDERFLOW)

            def body(j, s=s):
                step([(s, j * tq, tq, None)], bounded)
                return j - 1

            lax.while_loop(cond, body, first_block[s] - 1)
            o_ref[s * tq:(s + 1) * tq, :] = jnp.where(half[0], acc_ref[s, 0], acc_ref[s, 1]).astype(o_ref.dtype)

    bounded = bound_ref[0] < STATIC_LOGIT_LIMIT
    pl.when(bounded)(lambda: walk(True))
    pl.when(jnp.logical_not(bounded))(lambda: walk(False))


def _sb_attention(big, bound, gq2, gk2, batch, seq, tq=256):
    nsub = 2 if seq % (2 * tq) == 0 else 1
    tb = nsub * tq
    nq = seq // tb
    t = batch * seq
    lower = jnp.asarray(np.tril(np.ones((2 * tq, 2 * tq), np.float32)), BF16)
    kern = functools.partial(_sb_kernel, seq=seq, tq=tq, nsub=nsub, rows_per_pass=min(512, seq))
    return pl.pallas_call(
        kern,
        grid=(batch, N_HEADS // 2, nq),
        in_specs=[
            pl.BlockSpec(memory_space=pltpu.SMEM),
            pl.BlockSpec((tb, LANES), lambda b, p, i: (b * nq + i, C_SB + p)),
            pl.BlockSpec((seq, LANES), lambda b, p, i: (b, C_SB + 4 + p)),
            pl.BlockSpec((seq, LANES), lambda b, p, i: (b, C_SB + 8 + p)),
            pl.BlockSpec((1, LANES), lambda b, p, i: (0, 0)),
            pl.BlockSpec((1, LANES), lambda b, p, i: (0, 0)),
            pl.BlockSpec((2 * tq, 2 * tq), lambda b, p, i: (0, 0)),
        ],
        out_specs=pl.BlockSpec((tb, LANES), lambda b, p, i: (b * nq + i, p)),
        out_shape=jax.ShapeDtypeStruct((t, BRANCH_WIDTH), BF16),
        scratch_shapes=[
            pltpu.VMEM((seq, LANES), BF16),
            pltpu.VMEM((nsub, 2, tq, LANES), F32),
            pltpu.VMEM((nsub, 2, tq, 1), F32),
            pltpu.SMEM((nsub,), F32),
        ],
        compiler_params=_params(("arbitrary", "arbitrary", "arbitrary")),
        name="sb_attention",
    )(bound, big, big, big, gq2, gk2, lower)


def _gdn_prep_kernel(xq_ref, xk_ref, xv_ref, hq_ref, hk_ref, hv_ref, small_ref, cw_ref, alog_ref, dtb_ref,
                     grp_ref, eg_ref, eb_ref, tri_ref, q_ref, k_ref, v_ref, gc_ref, beta_ref, *, tiles_per_seq):
    first = (pl.program_id(0) % tiles_per_seq) == 0

    def conv_silu(x_ref, halo_ref, sec):
        halo = jnp.where(first, 0.0, halo_ref[...].astype(F32))
        xx = jnp.concatenate([halo, x_ref[...].astype(F32)], axis=0)
        hr = halo.shape[0]
        cols = slice(sec * BRANCH_WIDTH, (sec + 1) * BRANCH_WIDTH)
        y = cw_ref[GDN_CONV - 1:GDN_CONV, cols] * xx[hr:, :]
        for d in range(1, GDN_CONV):
            y = y + cw_ref[GDN_CONV - 1 - d:GDN_CONV - d, cols] * pltpu.roll(xx, d, 0)[hr:, :]
        return _silu(y)

    def l2(u):
        hi, lo = _split2(u * u)
        ss = _dot(hi, grp_ref[...]) + _dot(lo, grp_ref[...])
        return u * lax.rsqrt(ss + NORM_EPS)

    q_ref[...] = (l2(conv_silu(xq_ref, hq_ref, 0)) * (HEAD_DIM ** -0.5)).astype(q_ref.dtype)
    k_ref[...] = l2(conv_silu(xk_ref, hk_ref, 1)).astype(k_ref.dtype)
    v_ref[...] = conv_silu(xv_ref, hv_ref, 2).astype(v_ref.dtype)

    sm = small_ref[...]
    g = -jnp.exp(alog_ref[...]) * _softplus(sm + dtb_ref[...])
    g_wide = _dot_sel(g, eg_ref[...])
    gc_ref[...] = _sel_dot(tri_ref[...], g_wide)
    hi, lo = _split2(_sigmoid(sm))
    beta_ref[...] = _dot(hi, eb_ref[...]) + _dot(lo, eb_ref[...])


def _gdn_prep(big, small, conv_w, alog_row, dtb_row, batch, seq, tm=256):
    t = batch * seq
    halo_rows = 16
    grp = np.kron(np.eye(N_HEADS, dtype=np.float32), np.ones((HEAD_DIM, HEAD_DIM), np.float32))
    eg = np.zeros((LANES, BRANCH_WIDTH), np.float32)
    eb = np.zeros((LANES, BRANCH_WIDTH), np.float32)
    for h in range(N_HEADS):
        eg[L_GDNA + h, h * HEAD_DIM:(h + 1) * HEAD_DIM] = 1.0
        eb[L_GDNB + h, h * HEAD_DIM:(h + 1) * HEAD_DIM] = 1.0
    tri = np.kron(np.eye(tm // CHUNK, dtype=np.float32), np.tril(np.ones((CHUNK, CHUNK), np.float32)))
    wide = jax.ShapeDtypeStruct((t, BRANCH_WIDTH), F32)
    ratio = tm // halo_rows
    kern = functools.partial(_gdn_prep_kernel, tiles_per_seq=seq // tm)
    const = lambda shape: pl.BlockSpec(shape, lambda i: (0, 0))
    out_spec = pl.BlockSpec((tm, BRANCH_WIDTH), lambda i: (i, 0))
    sec0 = C_GDN * LANES // BRANCH_WIDTH
    x_specs = [pl.BlockSpec((tm, BRANCH_WIDTH), functools.partial(lambda i, s: (i, sec0 + s), s=s))
               for s in range(3)]
    halo_specs = [pl.BlockSpec((halo_rows, BRANCH_WIDTH),
                               functools.partial(lambda i, s: (jnp.maximum(i * ratio - 1, 0), sec0 + s), s=s))
                  for s in range(3)]
    return pl.pallas_call(
        kern,
        grid=(t // tm,),
        in_specs=x_specs + halo_specs + [
            pl.BlockSpec((tm, LANES), lambda i: (i, 0)),
            const((GDN_CONV, 3 * BRANCH_WIDTH)),
            const((1, LANES)),
            const((1, LANES)),
            const((BRANCH_WIDTH, BRANCH_WIDTH)),
            const((LANES, BRANCH_WIDTH)),
            const((LANES, BRANCH_WIDTH)),
            const((tm, tm)),
        ],
        out_specs=[out_spec] * 5,
        out_shape=[jax.ShapeDtypeStruct((t, BRANCH_WIDTH), BF16)] * 3 + [wide] * 2,
        compiler_params=_params(("arbitrary",)),
        name="gdn_prep",
    )(big, big, big, big, big, big, small, conv_w, alog_row, dtb_row, jnp.asarray(grp, BF16),
      jnp.asarray(eg, BF16), jnp.asarray(eb, BF16), jnp.asarray(tri, BF16))


def _stack_heads(x, half):
    return jnp.concatenate([jnp.where(half[0], x, 0.0), jnp.where(half[1], x, 0.0)], axis=0)


def _unstack(x):
    return x[:CHUNK, :] + x[CHUNK:, :]


def _gdn_kernel(q_ref, k_ref, v_ref, gc_ref, beta_ref, gain_ref, o_ref, state_ref, *, chunks):
    @pl.when(pl.program_id(1) == 0)
    def _():
        state_ref[...] = jnp.zeros_like(state_ref)

    half = _half_masks()
    n2 = 2 * CHUNK
    r = lax.broadcasted_iota(jnp.int32, (n2, n2), 0)
    c = lax.broadcasted_iota(jnp.int32, (n2, n2), 1)
    same = (r < CHUNK) == (c < CHUNK)
    lower = same & (c <= r)
    strict = same & (c < r)
    eye = (r == c).astype(F32)
    lane = _lane_iota((1, LANES))
    sub = lane % HEAD_DIM

    pairs = range(N_HEADS // 2)
    cps = [(ci, p) for ci in range(chunks) for p in pairs]
    win = {cp: (slice(cp[0] * CHUNK, (cp[0] + 1) * CHUNK), slice(cp[1] * LANES, (cp[1] + 1) * LANES))
           for cp in cps}

    diff_l, diff_r, kb_s, q_s, kk2, rhs, k_dec, q_dec, last = {}, {}, {}, {}, {}, {}, {}, {}, {}
    for cp in cps:
        rows, cols = win[cp]
        q2, k2, v2 = [ref[rows, cols].astype(F32) for ref in (q_ref, k_ref, v_ref)]
        gc2, beta2 = gc_ref[rows, cols], beta_ref[rows, cols]
        hi, lo, lo2 = [u.astype(F32) for u in _split3(gc2)]
        one = jnp.where((sub >= 3) & (sub < 6), 1.0, 0.0)
        la = jnp.where(sub == 0, hi, jnp.where(sub == 1, lo, jnp.where(sub == 2, lo2, one)))
        one = jnp.where(sub < 3, 1.0, 0.0)
        ra = jnp.where(sub == 3, -hi, jnp.where(sub == 4, -lo, jnp.where(sub == 5, -lo2, one))).astype(BF16)
        diff_l[cp] = _stack_heads(la, half).astype(BF16)
        diff_r[cp] = jnp.concatenate([ra, ra], axis=0)
        eg = jnp.exp(gc2)
        kb = k2 * beta2
        kk = k2.astype(BF16)
        kk2[cp] = jnp.concatenate([kk, kk], axis=0)
        kb_s[cp] = _stack_heads(kb, half).astype(BF16)
        q_s[cp] = _stack_heads(q2, half).astype(BF16)
        rhs[cp] = jnp.concatenate([_stack_heads(v2 * beta2, half), _stack_heads(kb * eg, half)],
                                  axis=1).astype(BF16)
        k_dec[cp] = (k2 * jnp.exp(gc2[CHUNK - 1:CHUNK, :] - gc2)).astype(BF16)
        q_dec[cp] = q2 * eg
        last[cp] = eg[CHUNK - 1:CHUNK, :]

    diff = {cp: _dot_nt(diff_l[cp], diff_r[cp]) for cp in cps}
    akk = {cp: _dot_nt(kb_s[cp], kk2[cp]) for cp in cps}
    aqk = {cp: _dot_nt(q_s[cp], kk2[cp]) for cp in cps}

    x, pw, intra = {}, {}, {}
    for cp in cps:
        decay = jnp.where(lower, jnp.exp(jnp.minimum(diff[cp], 0.0)), 0.0)
        a = jnp.where(strict, akk[cp] * decay, 0.0)
        intra[cp] = (aqk[cp] * decay).astype(BF16)
        x[cp] = eye - a
        pw[cp] = a

    for _ in range(5):
        for cp in cps:
            pb = pw[cp].astype(BF16)
            pw[cp] = _dot(pb, pb)
        for cp in cps:
            x[cp] = x[cp] + _dot(x[cp].astype(BF16), pw[cp].astype(BF16))

    uw = {cp: _dot(x[cp].astype(BF16), rhs[cp]) for cp in cps}
    u2 = {cp: _unstack(uw[cp][:, :LANES]) for cp in cps}
    w2 = {cp: _unstack(uw[cp][:, LANES:]) for cp in cps}

    uw_stack = {cp: jnp.concatenate([_stack_heads(u2[cp], half), _stack_heads(w2[cp], half)],
                                    axis=1).astype(BF16) for cp in cps}
    uw_flat = {cp: jnp.concatenate([u2[cp], w2[cp]], axis=1).astype(BF16) for cp in cps}
    iuw = {cp: _dot(intra[cp], uw_stack[cp]) for cp in cps}
    kuw = {cp: _dot_tn(k_dec[cp], uw_flat[cp]) for cp in cps}
    o_loc, lhs, s_add = {}, {}, {}
    for cp in cps:
        o_loc[cp] = _unstack(iuw[cp][:, :LANES])
        q_eff = q_dec[cp] - _unstack(iuw[cp][:, LANES:])
        m_eff = jnp.where(same, -kuw[cp][:, LANES:], 0.0)
        lhs[cp] = jnp.concatenate([q_eff, m_eff], axis=0).astype(BF16)
        s_add[cp] = jnp.where(same, kuw[cp][:, :LANES], 0.0)

    state = {p: state_ref[p] for p in pairs}
    for ci in range(chunks):
        res = {}
        for p in pairs:
            hi, lo = _split2(state[p])
            res[p] = _dot(lhs[(ci, p)], jnp.concatenate([hi, lo], axis=1))
        for p in pairs:
            cp = (ci, p)
            rr = res[p][:, :LANES] + res[p][:, LANES:]
            o2 = rr[:CHUNK, :] + o_loc[cp]
            state[p] = state[p] * last[cp] + rr[CHUNK:, :] + s_add[cp]
            rows, cols = win[cp]
            o_ref[rows, cols] = _pair_rms(o2, gain_ref[...], *half).astype(o_ref.dtype)
    for p in pairs:
        state_ref[p] = state[p]


def _gdn_scan(qn, kn, vv, gc, beta, gain2, batch, seq, chunks=4):
    t = batch * seq
    tm = chunks * CHUNK
    nb = seq // tm
    spec = pl.BlockSpec((tm, BRANCH_WIDTH), lambda b, n: (b * nb + n, 0))
    return pl.pallas_call(
        functools.partial(_gdn_kernel, chunks=chunks),
        grid=(batch, nb),
        in_specs=[spec] * 5 + [pl.BlockSpec((1, LANES), lambda b, n: (0, 0))],
        out_specs=spec,
        out_shape=jax.ShapeDtypeStruct((t, BRANCH_WIDTH), BF16),
        scratch_shapes=[pltpu.VMEM((N_HEADS // 2, LANES, LANES), F32)],
        compiler_params=_params(("arbitrary", "arbitrary")),
        name="gdn_scan",
    )(qn, kn, vv, gc, beta, gain2)


GLA_LEVELS = 6


def _gla_constants():
    n = CHUNK
    t = np.arange(n)
    blocks = []
    masks = []
    for lev in range(GLA_LEVELS):
        w = 1 << lev
        lower = (t // w) % 2 == 1
        mid_q = (t // w) * w
        cq = (lower[:, None] & (t[None, :] > mid_q[:, None]) & (t[None, :] <= t[:, None])).astype(np.float32)
        mid_k = (t // w + 1) * w
        ck = ((~lower)[:, None] & (t[None, :] > t[:, None]) & (t[None, :] <= mid_k[:, None])).astype(np.float32)
        blocks += [cq, ck]
        m = lower[:, None] & (~lower)[None, :] & ((t // (2 * w))[:, None] == (t // (2 * w))[None, :])
        masks.append(np.kron(np.eye(2), m.astype(np.float32)))
    masks.append(np.eye(2 * n, dtype=np.float32))
    tri = np.tril(np.ones((n, n), np.float32))
    rev = (t[None, :] > t[:, None]).astype(np.float32)
    blocks += [tri, rev]
    return np.concatenate(blocks, axis=0), np.stack(masks)


def _gla_kernel(x_ref, small_ref, w2_ref, bias_ref, sel_ref, mask_ref, gain_ref, o_ref, state_ref, *, chunks):
    @pl.when(pl.program_id(1) == 0)
    def _():
        state_ref[...] = jnp.zeros_like(state_ref)

    half = _half_masks()
    qk_w = GLA_HEADS * HEAD_DIM
    r = lax.broadcasted_iota(jnp.int32, (2 * LANES, LANES), 0)
    c = lax.broadcasted_iota(jnp.int32, (2 * LANES, LANES), 1)
    state_mask = (r < LANES) == (c < HEAD_DIM)
    zeros_v = jnp.zeros((CHUNK, LANES), F32)

    pairs = range(GLA_HEADS // 2)
    cps = [(ci, p) for ci in range(chunks) for p in pairs]
    crow = lambda ci: slice(ci * CHUNK, (ci + 1) * CHUNK)

    log_a = {}
    for ci in range(chunks):
        y = _dot(small_ref[crow(ci), :].astype(BF16), w2_ref[...]) + bias_ref[...]
        log_a[ci] = _log_sigmoid(y) * (1.0 / GLA_GATE_TAU)

    q2, k2, v_pair, la_split = {}, {}, {}, {}
    for ci, p in cps:
        rows = crow(ci)
        q2[ci, p] = x_ref[rows, p * LANES:(p + 1) * LANES].astype(F32) * (HEAD_DIM ** -0.5)
        k2[ci, p] = x_ref[rows, qk_w + p * LANES:qk_w + (p + 1) * LANES].astype(F32)
        v_pair[ci, p] = x_ref[rows, 2 * qk_w + 2 * p * LANES:2 * qk_w + 2 * (p + 1) * LANES]
        hi, lo = _split2(log_a[ci][:, p * LANES:(p + 1) * LANES])
        la_split[ci, p] = jnp.concatenate([hi, lo], axis=1)

    sums = {}
    for cp in cps:
        s2 = _dot(sel_ref[...], la_split[cp])
        sums[cp] = s2[:, :LANES] + s2[:, LANES:]

    a = {cp: None for cp in cps}
    for lev in range(GLA_LEVELS + 1):
        lhs, rhs = {}, {}
        for cp in cps:
            if lev < GLA_LEVELS:
                dq = sums[cp][(2 * lev) * CHUNK:(2 * lev + 1) * CHUNK, :]
                dk = sums[cp][(2 * lev + 1) * CHUNK:(2 * lev + 2) * CHUNK, :]
                ql, kl = q2[cp] * jnp.exp(dq), k2[cp] * jnp.exp(dk)
            else:
                ql, kl = q2[cp], k2[cp]
            kb = kl.astype(BF16)
            lhs[cp] = _stack_heads(ql, half).astype(BF16)
            rhs[cp] = jnp.concatenate([kb, kb], axis=0)
        prod = {cp: _dot_nt(lhs[cp], rhs[cp]) for cp in cps}
        for cp in cps:
            term = prod[cp] * mask_ref[lev]
            a[cp] = term if a[cp] is None else a[cp] + term

    v_stack, q_dec, k_dec, b_last = {}, {}, {}, {}
    for cp in cps:
        b = sums[cp][2 * GLA_LEVELS * CHUNK:(2 * GLA_LEVELS + 1) * CHUNK, :]
        b_rev = sums[cp][(2 * GLA_LEVELS + 1) * CHUNK:, :]
        vf = v_pair[cp].astype(F32)
        v_stack[cp] = jnp.concatenate(
            [jnp.concatenate([vf[:, :LANES], zeros_v], axis=1),
             jnp.concatenate([zeros_v, vf[:, LANES:]], axis=1)], axis=0).astype(BF16)
        q_dec[cp] = (q2[cp] * jnp.exp(b)).astype(BF16)
        k_dec[cp] = (k2[cp] * jnp.exp(b_rev)).astype(BF16)
        b_last[cp] = jnp.exp(b[CHUNK - 1:CHUNK, :])
    o_loc = {cp: _unstack(_dot(a[cp].astype(BF16), v_stack[cp])) for cp in cps}
    upd = {cp: jnp.where(state_mask, _dot_tn(v_pair[cp], k_dec[cp]), 0.0) for cp in cps}

    state = {p: state_ref[p] for p in pairs}
    for ci in range(chunks):
        o_pair = {p: o_loc[ci, p] + _dot_nt(q_dec[ci, p], state[p].astype(BF16)) for p in pairs}
        for p in pairs:
            state[p] = state[p] * b_last[ci, p] + upd[ci, p]
            for h in range(2):
                oh = o_pair[p][:, h * LANES:(h + 1) * LANES]
                ms = jnp.mean(oh * oh, axis=-1, keepdims=True)
                col0 = (2 * p + h) * LANES
                o_ref[crow(ci), col0:col0 + LANES] = (
                    oh * lax.rsqrt(ms + NORM_EPS) * gain_ref[...]).astype(o_ref.dtype)
    for p in pairs:
        state_ref[p] = state[p]


def _gla(big, small, w2_ext, bias_row, gain_row, batch, seq, chunks=8):
    t = batch * seq
    tm = chunks * CHUNK
    nb = seq // tm
    sel, masks = _gla_constants()
    width = 2 * GLA_HEADS * HEAD_DIM + BRANCH_WIDTH
    const2 = lambda shape: pl.BlockSpec(shape, lambda b, n: (0, 0))
    return pl.pallas_call(
        functools.partial(_gla_kernel, chunks=chunks),
        grid=(batch, nb),
        in_specs=[
            pl.BlockSpec((tm, width), lambda b, n: (b * nb + n, C_GLA * LANES // width)),
            pl.BlockSpec((tm, LANES), lambda b, n: (b * nb + n, 0)),
            const2((LANES, GLA_HEADS * HEAD_DIM)),
            const2((1, GLA_HEADS * HEAD_DIM)),
            const2(sel.shape),
            pl.BlockSpec(masks.shape, lambda b, n: (0, 0, 0)),
            const2((1, LANES)),
        ],
        out_specs=pl.BlockSpec((tm, BRANCH_WIDTH), lambda b, n: (b * nb + n, 0)),
        out_shape=jax.ShapeDtypeStruct((t, BRANCH_WIDTH), BF16),
        scratch_shapes=[pltpu.VMEM((GLA_HEADS // 2, 2 * LANES, LANES), F32)],
        compiler_params=_params(("arbitrary", "arbitrary")),
        name="gla",
    )(big, small, w2_ext, bias_row, jnp.asarray(sel, BF16), jnp.asarray(masks, F32), gain_row)


def _merge_kernel(x_ref, o0_ref, o1_ref, o2_ref, o3_ref, z_ref, ml_ref, gb_ref, wb_ref, wo_ref, out_ref):
    merged = None
    for n, o_ref in enumerate((o0_ref, o1_ref, o2_ref, o3_ref)):
        z = z_ref[:, n * BRANCH_WIDTH:(n + 1) * BRANCH_WIDTH].astype(F32)
        br = (o_ref[...].astype(F32) * _silu(z)).astype(BF16)
        proj = _dot(br, wb_ref[n])
        gate = 1.0 / (1.0 + jnp.exp2(ml_ref[:, n * D_MODEL:(n + 1) * D_MODEL].astype(F32)
                                     + gb_ref[:, n * D_MODEL:(n + 1) * D_MODEL]))
        term = proj * gate
        merged = term if merged is None else merged + term
    out_ref[...] = x_ref[...] + _dot(merged.astype(BF16), wo_ref[...])


def _merge(x2, outs, big, gate_bias_row, w_branch, w_out, tm=256):
    t = x2.shape[0]
    zw = N_BRANCH * BRANCH_WIDTH
    mw = N_BRANCH * D_MODEL
    o_spec = pl.BlockSpec((tm, BRANCH_WIDTH), lambda i: (i, 0))
    return pl.pallas_call(
        _merge_kernel,
        grid=(t // tm,),
        in_specs=[
            pl.BlockSpec((tm, D_MODEL), lambda i: (i, 0)),
            o_spec, o_spec, o_spec, o_spec,
            pl.BlockSpec((tm, zw), lambda i: (i, C_Z * LANES // zw)),
            pl.BlockSpec((tm, mw), lambda i: (i, C_MERGE * LANES // mw)),
            pl.BlockSpec((1, mw), lambda i: (0, 0)),
            pl.BlockSpec((N_BRANCH, BRANCH_WIDTH, D_MODEL), lambda i: (0, 0, 0), pipeline_mode=pl.Buffered(1)),
            pl.BlockSpec((D_MODEL, D_MODEL), lambda i: (0, 0), pipeline_mode=pl.Buffered(1)),
        ],
        out_specs=pl.BlockSpec((tm, D_MODEL), lambda i: (i, 0)),
        out_shape=jax.ShapeDtypeStruct((t, D_MODEL), F32),
        compiler_params=_params(("arbitrary",)),
        name="merge_out",
    )(x2, *outs, big, big, gate_bias_row, w_branch, w_out)


def _reorder_w_in(w_in):
    sizes = (3 * BRANCH_WIDTH, N_HEADS, BRANCH_WIDTH, 3 * BRANCH_WIDTH, BRANCH_WIDTH,
             3 * BRANCH_WIDTH, N_HEADS, N_HEADS, BRANCH_WIDTH,
             2 * GLA_HEADS * HEAD_DIM, BRANCH_WIDTH, GLA_GATE_RANK, BRANCH_WIDTH, N_BRANCH * D_MODEL)
    (fox_qkv, fox_f, fox_z, sb_qkv, sb_z, gdn_qkv, gdn_a, gdn_b, gdn_z,
     gla_qk, gla_v, gla_g, gla_z, merge) = jnp.split(w_in, np.cumsum(sizes)[:-1].tolist(), axis=1)
    big = jnp.concatenate([merge * NEG_LOG2E, fox_z, sb_z, gdn_z, gla_z, gla_qk, gla_v, gdn_qkv, fox_qkv, sb_qkv],
                          axis=1)
    pad = jnp.zeros((w_in.shape[0], LANES - (3 * N_HEADS + GLA_GATE_RANK)), w_in.dtype)
    small = jnp.concatenate([fox_f, gdn_a, gdn_b, gla_g, pad], axis=1)
    return big.astype(BF16), small.astype(BF16)


def _lane_row(values, offset):
    row = jnp.zeros((1, LANES), F32)
    return row.at[0, offset:offset + values.shape[0]].set(values.astype(F32))


def _layer(x2, batch, seq, norm_g, w_in, fox_f_bias, fox_q_norm, fox_k_norm, sb_q_norm, sb_k_norm,
           gdn_conv_w, gdn_a_log, gdn_dt_bias, gdn_o_norm, gla_gate_w2, gla_gate_bias, gla_o_norm,
           gate_bias, w_branch, w_out):
    w_big, w_small = _reorder_w_in(w_in)
    g_row = norm_g.reshape(1, D_MODEL).astype(F32)
    t = batch * seq
    big, small = _norm_proj(x2, g_row, w_big, w_small, min(2048, t), 512)

    pair = lambda g: jnp.tile(g.astype(F32), 2).reshape(1, LANES)

    c = _fox_cum(small, _lane_row(fox_f_bias, L_FOXF), batch, seq)
    tq = tk = min(512, seq)
    c3 = c.reshape(batch, seq, LANES)[:, :, :N_HEADS]
    first = jnp.transpose(c3[:, 0::tq, :], (0, 2, 1))
    last = jnp.transpose(c3[:, tk - 1::tk, :], (0, 2, 1))
    logit_bound = lambda gq, gk: 1.02 * HEAD_DIM ** 0.5 * jnp.max(jnp.abs(gq)) * jnp.max(jnp.abs(gk))
    qk_bound = logit_bound(fox_q_norm, fox_k_norm)
    thr = EXP_UNDERFLOW + 2.0 * qk_bound + 1.0
    tab = jnp.concatenate([first.reshape(-1), last.reshape(-1), thr.reshape(1), qk_bound.reshape(1)]).astype(F32)
    o_fox = _fox_attention(big, c, tab, pair(fox_q_norm), pair(fox_k_norm), batch, seq, tq, tk)

    o_sb = _sb_attention(big, logit_bound(sb_q_norm, sb_k_norm).reshape(1).astype(F32),
                         pair(sb_q_norm), pair(sb_k_norm), batch, seq, min(256, seq))

    qn, kn, vv, gc, beta = _gdn_prep(big, small, gdn_conv_w.astype(F32), _lane_row(gdn_a_log, L_GDNA),
                                     _lane_row(gdn_dt_bias, L_GDNA), batch, seq, min(256, seq))
    o_gdn = _gdn_scan(qn, kn, vv, gc, beta, pair(gdn_o_norm), batch, seq)

    w2_ext = jnp.zeros((LANES, GLA_HEADS * HEAD_DIM), F32).at[L_GLAG:L_GLAG + GLA_GATE_RANK].set(
        gla_gate_w2.astype(F32)).astype(BF16)
    o_gla = _gla(big, small, w2_ext, gla_gate_bias.reshape(1, -1).astype(F32),
                 gla_o_norm.reshape(1, LANES).astype(F32), batch, seq)

    return _merge(x2, (o_fox, o_sb, o_gdn, o_gla), big, gate_bias.reshape(1, -1).astype(F32) * NEG_LOG2E,
                  w_branch.astype(BF16), w_out.astype(BF16), min(512, t))


def kernel(x, norm_g, w_in, fox_f_bias, fox_q_norm, fox_k_norm, sb_q_norm, sb_k_norm, gdn_conv_w,
           gdn_a_log, gdn_dt_bias, gdn_o_norm, gla_gate_w2, gla_gate_bias, gla_o_norm, gate_bias,
           w_branch, w_out):
    batch, seq, d = x.shape
    x2 = x.reshape(batch * seq, d)
    params = (norm_g, w_in, fox_f_bias, fox_q_norm, fox_k_norm, sb_q_norm, sb_k_norm, gdn_conv_w,
              gdn_a_log, gdn_dt_bias, gdn_o_norm, gla_gate_w2, gla_gate_bias, gla_o_norm, gate_bias,
              w_branch, w_out)
    for l in range(norm_g.shape[0]):
        x2 = _layer(x2, batch, seq, *[p[l] for p in params])
    return x2.reshape(batch, seq, d)
```

```python
import functools

import numpy as np
import jax
import jax.numpy as jnp
from jax import lax
from jax.experimental import pallas as pl
from jax.experimental.pallas import tpu as pltpu

F32 = jnp.float32
BF16 = jnp.bfloat16

LANES = 128
HEAD_DIM = 64
D_MODEL = 1024
BRANCH_WIDTH = 512
N_BRANCH = 4
N_HEADS = 8
GLA_HEADS = 4
GLA_GATE_RANK = 16
GLA_GATE_TAU = 16.0
GDN_CONV = 4
CHUNK = 64
NORM_EPS = 1e-6
VMEM_LIMIT = 48 * 1024 * 1024
EXP_UNDERFLOW = 104.0
NEG_BIG = -1e30
NEG_LOG2E = -1.4426950408889634
STATIC_LOGIT_LIMIT = 20.0

C_MERGE, C_Z, C_GLA, C_GDN, C_FOX, C_SB, C_END = 0, 32, 48, 56, 68, 80, 92
N_BIG = C_END * LANES
L_FOXF, L_GDNA, L_GDNB, L_GLAG = 0, 8, 16, 24


def _dot(a, b):
    return jnp.dot(a, b, preferred_element_type=F32)


def _dot_nt(a, b):
    return lax.dot_general(a, b, (((1,), (1,)), ((), ())), preferred_element_type=F32)


def _dot_tn(a, b):
    return lax.dot_general(a, b, (((0,), (0,)), ((), ())), preferred_element_type=F32)


def _split2(x):
    hi = x.astype(BF16)
    lo = (x - hi.astype(F32)).astype(BF16)
    return hi, lo


def _split3(x):
    hi = x.astype(BF16)
    r = x - hi.astype(F32)
    lo = r.astype(BF16)
    lo2 = (r - lo.astype(F32)).astype(BF16)
    return hi, lo, lo2


def _sel_dot(sel, x):
    hi, lo, lo2 = _split3(x)
    return _dot(sel, hi) + _dot(sel, lo) + _dot(sel, lo2)


def _dot_sel(x, sel):
    hi, lo, lo2 = _split3(x)
    return _dot(hi, sel) + _dot(lo, sel) + _dot(lo2, sel)


def _softplus(x):
    return jnp.maximum(x, 0.0) + jnp.log(1.0 + jnp.exp(-jnp.abs(x)))


def _log_sigmoid(x):
    return jnp.minimum(x, 0.0) - jnp.log(1.0 + jnp.exp(-jnp.abs(x)))


def _sigmoid(x):
    return 1.0 / (1.0 + jnp.exp(-x))


def _silu(x):
    return x / (1.0 + jnp.exp(-x))


def _lane_iota(shape):
    return lax.broadcasted_iota(jnp.int32, shape, len(shape) - 1)


def _half_masks():
    lane = _lane_iota((1, LANES))
    return lane < HEAD_DIM, lane >= HEAD_DIM


def _params(semantics):
    return pltpu.CompilerParams(dimension_semantics=semantics, vmem_limit_bytes=VMEM_LIMIT)


def _proj_kernel(x_ref, g_ref, w_ref, ws_ref, o_ref, os_ref, h_ref):
    @pl.when(pl.program_id(1) == 0)
    def _():
        x = x_ref[...]
        ms = jnp.mean(x * x, axis=-1, keepdims=True)
        h_ref[...] = (x * lax.rsqrt(ms + NORM_EPS) * g_ref[...]).astype(BF16)
        os_ref[...] = _dot(h_ref[...], ws_ref[...])

    o_ref[...] = _dot(h_ref[...], w_ref[...]).astype(o_ref.dtype)


def _norm_proj(x2, g, w, w_small, tm, tn):
    t, d = x2.shape
    n = w.shape[1]
    return pl.pallas_call(
        _proj_kernel,
        grid=(t // tm, n // tn),
        in_specs=[
            pl.BlockSpec((tm, d), lambda i, j: (i, 0)),
            pl.BlockSpec((1, d), lambda i, j: (0, 0)),
            pl.BlockSpec((d, tn), lambda i, j: (0, j)),
            pl.BlockSpec((d, LANES), lambda i, j: (0, 0)),
        ],
        out_specs=[pl.BlockSpec((tm, tn), lambda i, j: (i, j)),
                   pl.BlockSpec((tm, LANES), lambda i, j: (i, 0))],
        out_shape=[jax.ShapeDtypeStruct((t, n), BF16), jax.ShapeDtypeStruct((t, LANES), F32)],
        scratch_shapes=[pltpu.VMEM((tm, d), BF16)],
        compiler_params=_params(("arbitrary", "arbitrary")),
        name="norm_proj",
    )(x2, g, w, w_small)


def _fox_cum_kernel(f_ref, bias_ref, tri_ref, c_ref, carry_ref):
    @pl.when(pl.program_id(1) == 0)
    def _():
        carry_ref[...] = jnp.zeros_like(carry_ref)

    lf = _log_sigmoid(f_ref[...] + bias_ref[...])
    cs = _sel_dot(tri_ref[...], lf) + carry_ref[...]
    c_ref[...] = cs
    carry_ref[...] = cs[-1:, :]


def _fox_cum(small, bias_row, batch, seq, tr=512):
    tri = jnp.asarray(np.tril(np.ones((tr, tr), np.float32)), BF16)
    nb = seq // tr
    return pl.pallas_call(
        _fox_cum_kernel,
        grid=(batch, nb),
        in_specs=[
            pl.BlockSpec((tr, LANES), lambda b, r: (b * nb + r, 0)),
            pl.BlockSpec((1, LANES), lambda b, r: (0, 0)),
            pl.BlockSpec((tr, tr), lambda b, r: (0, 0)),
        ],
        out_specs=pl.BlockSpec((tr, LANES), lambda b, r: (b * nb + r, 0)),
        out_shape=jax.ShapeDtypeStruct(small.shape, F32),
        scratch_shapes=[pltpu.VMEM((1, LANES), F32)],
        compiler_params=_params(("arbitrary", "arbitrary")),
        name="fox_cum",
    )(small, bias_row, tri)


def _pair_rms(x, gain, half0, half1):
    sq = x * x
    ms0 = jnp.sum(jnp.where(half0, sq, 0.0), axis=-1, keepdims=True) * (1.0 / HEAD_DIM)
    ms1 = jnp.sum(jnp.where(half1, sq, 0.0), axis=-1, keepdims=True) * (1.0 / HEAD_DIM)
    r = jnp.where(half0, lax.rsqrt(ms0 + NORM_EPS), lax.rsqrt(ms1 + NORM_EPS))
    return x * r * gain


def _lane_pick(x, idx):
    lane = _lane_iota((1, LANES))
    return jnp.sum(jnp.where(lane == idx, x, 0.0), axis=-1, keepdims=True)


def _aug_lanes(c_col, base, key_side):
    lane = _lane_iota((1, LANES))
    hi, lo, lo2 = [v.astype(F32) for v in _split3(c_col)]
    if key_side:
        hi, lo, lo2 = -hi, -lo, -lo2
        vals, ones = (base + 3, base + 4, base + 5), (base, base + 1, base + 2)
    else:
        vals, ones = (base, base + 1, base + 2), (base + 3, base + 4, base + 5)
    out = jnp.where((lane == ones[0]) | (lane == ones[1]) | (lane == ones[2]), 1.0, 0.0)
    out = jnp.where(lane == vals[0], hi, out)
    out = jnp.where(lane == vals[1], lo, out)
    out = jnp.where(lane == vals[2], lo2, out)
    return out


def _fox_kernel(tab_ref, q_ref, k_ref, v_ref, cq_ref, ck_ref, gq_ref, gk_ref, o_ref,
                kaug_ref, vaug_ref, acc_ref, m_ref, l_ref, *, seq, tq, tk, rows_per_pass):
    b, p, i = pl.program_id(0), pl.program_id(1), pl.program_id(2)
    nq, nk = seq // tq, seq // tk
    nb = pl.num_programs(0)
    half = _half_masks()
    lane = _lane_iota((1, LANES))
    scale = HEAD_DIM ** -0.5
    spare = [HEAD_DIM * (1 - h) for h in range(2)]

    @pl.when(i == 0)
    def _():
        def fill(r, carry):
            rows = pl.ds(pl.multiple_of(r * rows_per_pass, rows_per_pass), rows_per_pass)
            kn = _pair_rms(k_ref[rows, :].astype(F32), gk_ref[...], *half)
            cc = ck_ref[rows, :]
            vv = v_ref[rows, :]
            for h in range(2):
                aug = _aug_lanes(_lane_pick(cc, 2 * p + h), spare[h], True)
                kaug_ref[h, rows, :] = jnp.where(half[h], kn, aug).astype(BF16)
                ones = jnp.where(lane == spare[h], 1.0, 0.0).astype(BF16)
                vaug_ref[h, rows, :] = jnp.where(half[h], vv, ones)
            return carry
        lax.fori_loop(0, seq // rows_per_pass, fill, 0)

    first_off = N_HEADS * nb * nq
    thr = tab_ref[first_off + N_HEADS * nb * nk]
    bound = tab_ref[first_off + N_HEADS * nb * nk + 1]
    fast = bound < STATIC_LOGIT_LIMIT

    qn = _pair_rms(q_ref[...].astype(F32), gq_ref[...], *half) * scale
    cq = cq_ref[...]
    tile_i = i // (tk // tq)
    row = lax.broadcasted_iota(jnp.int32, (tq, tk), 0) + (i % (tk // tq)) * tq
    col = lax.broadcasted_iota(jnp.int32, (tq, tk), 1)
    causal = col <= row

    c_first = [tab_ref[(b * N_HEADS + 2 * p + h) * nq + i] for h in range(2)]

    def cond(j):
        jj = jnp.maximum(j, 0)
        need = [c_first[h] - tab_ref[first_off + (b * N_HEADS + 2 * p + h) * nk + jj] >= -thr for h in range(2)]
        return jnp.logical_and(j >= 0, jnp.logical_or(need[0], need[1]))

    def walk(tile):
        tile(tile_i, True)

        def body(j):
            tile(j, False)
            return j - 1
        lax.while_loop(cond, body, tile_i - 1)

    def q_aug(shift):
        return [jnp.where(half[h], qn, _aug_lanes(_lane_pick(cq, 2 * p + h) - shift, spare[h], False)).astype(BF16)
                for h in range(2)]

    @pl.when(fast)
    def _():
        qa = q_aug(bound)
        acc_ref[...] = jnp.zeros_like(acc_ref)

        def tile(j, masked):
            rows = pl.ds(pl.multiple_of(j * tk, tk), tk)
            s = [_dot_nt(qa[h], kaug_ref[h, rows, :]) for h in range(2)]
            if masked:
                s = [jnp.where(causal, sh, NEG_BIG) for sh in s]
            pr = [jnp.exp(sh).astype(BF16) for sh in s]
            for h in range(2):
                acc_ref[h] = acc_ref[h] + _dot(pr[h], vaug_ref[h, rows, :])

        walk(tile)
        out = [acc_ref[h] / acc_ref[h][:, spare[h]:spare[h] + 1] for h in range(2)]
        o_ref[...] = jnp.where(half[0], out[0], out[1]).astype(o_ref.dtype)

    @pl.when(jnp.logical_not(fast))
    def _():
        qa = q_aug(0.0)
        acc_ref[...] = jnp.zeros_like(acc_ref)
        m_ref[...] = jnp.full_like(m_ref, NEG_BIG)
        l_ref[...] = jnp.zeros_like(l_ref)

        def tile(j, masked):
            rows = pl.ds(pl.multiple_of(j * tk, tk), tk)
            for h in range(2):
                s = _dot_nt(qa[h], kaug_ref[h, rows, :])
                if masked:
                    s = jnp.where(causal, s, NEG_BIG)
                m_prev = m_ref[h]
                m_new = jnp.maximum(m_prev, jnp.max(s, axis=1, keepdims=True))
                alpha = jnp.exp(m_prev - m_new)
                pr = jnp.exp(s - m_new)
                l_ref[h] = alpha * l_ref[h] + jnp.sum(pr, axis=1, keepdims=True)
                acc_ref[h] = alpha * acc_ref[h] + _dot(pr.astype(BF16), vaug_ref[h, rows, :])
                m_ref[h] = m_new

        walk(tile)
        out = [acc_ref[h] / l_ref[h] for h in range(2)]
        o_ref[...] = jnp.where(half[0], out[0], out[1]).astype(o_ref.dtype)


def _fox_attention(big, c, tab, gq2, gk2, batch, seq, tq, tk):
    nq = seq // tq
    t = batch * seq
    kern = functools.partial(_fox_kernel, seq=seq, tq=tq, tk=tk, rows_per_pass=min(512, seq))
    grid_spec = pltpu.PrefetchScalarGridSpec(
        num_scalar_prefetch=1,
        grid=(batch, N_HEADS // 2, nq),
        in_specs=[
            pl.BlockSpec((tq, LANES), lambda b, p, i, tab: (b * nq + i, C_FOX + p)),
            pl.BlockSpec((seq, LANES), lambda b, p, i, tab: (b, C_FOX + 4 + p)),
            pl.BlockSpec((seq, LANES), lambda b, p, i, tab: (b, C_FOX + 8 + p)),
            pl.BlockSpec((tq, LANES), lambda b, p, i, tab: (b * nq + i, 0)),
            pl.BlockSpec((seq, LANES), lambda b, p, i, tab: (b, 0)),
            pl.BlockSpec((1, LANES), lambda b, p, i, tab: (0, 0)),
            pl.BlockSpec((1, LANES), lambda b, p, i, tab: (0, 0)),
        ],
        out_specs=pl.BlockSpec((tq, LANES), lambda b, p, i, tab: (b * nq + i, p)),
        scratch_shapes=[
            pltpu.VMEM((2, seq, LANES), BF16),
            pltpu.VMEM((2, seq, LANES), BF16),
            pltpu.VMEM((2, tq, LANES), F32),
            pltpu.VMEM((2, tq, 1), F32),
            pltpu.VMEM((2, tq, 1), F32),
        ],
    )
    return pl.pallas_call(
        kern,
        grid_spec=grid_spec,
        out_shape=jax.ShapeDtypeStruct((t, BRANCH_WIDTH), BF16),
        compiler_params=_params(("arbitrary", "arbitrary", "arbitrary")),
        name="fox_attention",
    )(tab, big, big, big, c, c, gq2, gk2)


SB_MASKED_LOGIT = -1e4


def _sb_kernel(bound_ref, q_ref, k_ref, v_ref, gq_ref, gk_ref, lower_ref, o_ref,
               kn_ref, acc_ref, run_ref, flag_ref, *, seq, tq, nsub, rows_per_pass):
    i = pl.program_id(2)
    half = _half_masks()
    scale = HEAD_DIM ** -0.5

    @pl.when(i == 0)
    def _():
        def fill(r, carry):
            rows = pl.ds(pl.multiple_of(r * rows_per_pass, rows_per_pass), rows_per_pass)
            kn_ref[rows, :] = _pair_rms(k_ref[rows, :].astype(F32), gk_ref[...], *half).astype(BF16)
            return carry
        lax.fori_loop(0, seq // rows_per_pass, fill, 0)

    qn = _pair_rms(q_ref[...].astype(F32), gq_ref[...], *half) * scale
    qh = [[jnp.where(half[h], qn[s * tq:(s + 1) * tq], 0.0).astype(BF16) for h in range(2)] for s in range(nsub)]
    acc_ref[...] = jnp.zeros_like(acc_ref)
    run_ref[...] = jnp.zeros_like(run_ref)
    block = [i * nsub + s for s in range(nsub)]
    first_block = [jnp.maximum(blk - 1, 0) for blk in block]

    def step(tiles, bounded):
        kk, vv, low, z = {}, {}, {}, {}
        for s, start, width, offset in tiles:
            rows = pl.ds(pl.multiple_of(start, tq), width)
            kk[s], vv[s], low[s] = kn_ref[rows, :], v_ref[rows, :], lower_ref[:width, :width]
        for s, start, width, offset in tiles:
            for h in range(2):
                z[s, h] = _dot_nt(qh[s][h], kk[s])
        for s, start, width, offset in tiles:
            if offset is not None:
                row = lax.broadcasted_iota(jnp.int32, (tq, width), 0) + offset
                col = lax.broadcasted_iota(jnp.int32, (tq, width), 1)
                for h in range(2):
                    z[s, h] = jnp.where(col < row, z[s, h], SB_MASKED_LOGIT)
        keys = list(z)
        if bounded:
            log_rest = {k: -jnp.log(1.0 + jnp.exp(z[k])) for k in keys}
        else:
            log_rest = {k: -_softplus(z[k]) for k in keys}
        inner = {k: _dot(log_rest[k].astype(BF16), low[k[0]]) for k in keys}
        w = {k: jnp.exp((z[k] + run_ref[k[0], k[1]]) + inner[k]).astype(BF16) for k in keys}
        for k in keys:
            acc_ref[k[0], k[1]] = acc_ref[k[0], k[1]] + _dot(w[k], vv[k[0]])
        run = {k: run_ref[k[0], k[1]] + jnp.sum(log_rest[k], axis=1, keepdims=True) for k in keys}
        for k in keys:
            run_ref[k[0], k[1]] = run[k]
        for s, start, width, offset in tiles:
            flag_ref[s] = jnp.maximum(jnp.max(run[s, 0]), jnp.max(run[s, 1]))

    def walk(bounded):
        step([(s, first_block[s] * tq, 2 * tq, (block[s] - first_block[s]) * tq) for s in range(nsub)], bounded)
        for s in range(nsub):
            def cond(j, s=s):
                return jnp.logical_and(j >= 0, flag_ref[s] > -EXP_UNDERFLOW)

            def body(j, s=s):
                step([(s, j * tq, tq, None)], bounded)
                return j - 1

            lax.while_loop(cond, body, first_block[s] - 1)
            o_ref[s * tq:(s + 1) * tq, :] = jnp.where(half[0], acc_ref[s, 0], acc_ref[s, 1]).astype(o_ref.dtype)

    bounded = bound_ref[0] < STATIC_LOGIT_LIMIT
    pl.when(bounded)(lambda: walk(True))
    pl.when(jnp.logical_not(bounded))(lambda: walk(False))


def _sb_attention(big, bound, gq2, gk2, batch, seq, tq=256):
    nsub = 2 if seq % (2 * tq) == 0 else 1
    tb = nsub * tq
    nq = seq // tb
    t = batch * seq
    lower = jnp.asarray(np.tril(np.ones((2 * tq, 2 * tq), np.float32)), BF16)
    kern = functools.partial(_sb_kernel, seq=seq, tq=tq, nsub=nsub, rows_per_pass=min(512, seq))
    return pl.pallas_call(
        kern,
        grid=(batch, N_HEADS // 2, nq),
        in_specs=[
            pl.BlockSpec(memory_space=pltpu.SMEM),
            pl.BlockSpec((tb, LANES), lambda b, p, i: (b * nq + i, C_SB + p)),
            pl.BlockSpec((seq, LANES), lambda b, p, i: (b, C_SB + 4 + p)),
            pl.BlockSpec((seq, LANES), lambda b, p, i: (b, C_SB + 8 + p)),
            pl.BlockSpec((1, LANES), lambda b, p, i: (0, 0)),
            pl.BlockSpec((1, LANES), lambda b, p, i: (0, 0)),
            pl.BlockSpec((2 * tq, 2 * tq), lambda b, p, i: (0, 0)),
        ],
        out_specs=pl.BlockSpec((tb, LANES), lambda b, p, i: (b * nq + i, p)),
        out_shape=jax.ShapeDtypeStruct((t, BRANCH_WIDTH), BF16),
        scratch_shapes=[
            pltpu.VMEM((seq, LANES), BF16),
            pltpu.VMEM((nsub, 2, tq, LANES), F32),
            pltpu.VMEM((nsub, 2, tq, 1), F32),
            pltpu.SMEM((nsub,), F32),
        ],
        compiler_params=_params(("arbitrary", "arbitrary", "arbitrary")),
        name="sb_attention",
    )(bound, big, big, big, gq2, gk2, lower)


GDN_HALO = 16


def _gdn_prep_vpu(first, xq_ref, xk_ref, xv_ref, hq_ref, hk_ref, hv_ref, small_ref, cw_ref, alog_ref, dtb_ref,
                  grp_ref, eg_ref, eb_ref, tri_ref):
    def conv_silu(x_ref, halo_ref, sec):
        halo = jnp.where(first, 0.0, halo_ref[...].astype(F32))
        xx = jnp.concatenate([halo, x_ref[...].astype(F32)], axis=0)
        cols = slice(sec * BRANCH_WIDTH, (sec + 1) * BRANCH_WIDTH)
        y = cw_ref[GDN_CONV - 1:GDN_CONV, cols] * xx[GDN_HALO:, :]
        for d in range(1, GDN_CONV):
            y = y + cw_ref[GDN_CONV - 1 - d:GDN_CONV - d, cols] * pltpu.roll(xx, d, 0)[GDN_HALO:, :]
        return _silu(y)

    q, k, v = conv_silu(xq_ref, hq_ref, 0), conv_silu(xk_ref, hk_ref, 1), conv_silu(xv_ref, hv_ref, 2)
    sm = small_ref[...]
    g = -jnp.exp(alog_ref[...]) * _softplus(sm + dtb_ref[...])
    return dict(q=q, k=k, v=v, q_sq=_split2(q * q), k_sq=_split2(k * k), g=_split3(g), beta=_split2(_sigmoid(sm)))


def _gdn_prep_mxu(part, grp_ref, eg_ref, eb_ref, tri_ref):
    def l2(u, sq):
        ss = _dot(sq[0], grp_ref[...]) + _dot(sq[1], grp_ref[...])
        return u * lax.rsqrt(ss + NORM_EPS)

    q = l2(part["q"], part["q_sq"]) * (HEAD_DIM ** -0.5)
    k = l2(part["k"], part["k_sq"])
    g_wide = sum(_dot(term, eg_ref[...]) for term in part["g"])
    gc = _sel_dot(tri_ref[...], g_wide)
    beta = sum(_dot(term, eb_ref[...]) for term in part["beta"])
    return q, k, part["v"], gc, beta


def _stack_heads(x, half):
    return jnp.concatenate([jnp.where(half[0], x, 0.0), jnp.where(half[1], x, 0.0)], axis=0)


def _unstack(x):
    return x[:CHUNK, :] + x[CHUNK:, :]


GDN_PREP_INPUTS = 7
GDN_PREP_CONSTS = 7


def _gdn_kernel(*refs, chunks):
    tile = refs[:GDN_PREP_INPUTS]
    consts = refs[GDN_PREP_INPUTS:GDN_PREP_INPUTS + GDN_PREP_CONSTS]
    gain_ref, o_ref, state_ref = refs[GDN_PREP_INPUTS + GDN_PREP_CONSTS:]
    pairs = range(N_HEADS // 2)
    first = pl.program_id(1) == 0

    @pl.when(first)
    def _():
        state_ref[...] = jnp.zeros_like(state_ref)

    whole = _gdn_prep_mxu(_gdn_prep_vpu(first, *tile, *consts), *consts[3:])
    vals = [[a[c * CHUNK:(c + 1) * CHUNK, :] for a in whole] for c in range(chunks)]
    state = _gdn_scan_tile(vals, {p: state_ref[p] for p in pairs}, gain_ref, o_ref, chunks)
    for p in pairs:
        state_ref[p] = state[p]


def _gdn_scan_tile(vals, state, gain_ref, o_ref, chunks):
    half = _half_masks()
    n2 = 2 * CHUNK
    r = lax.broadcasted_iota(jnp.int32, (n2, n2), 0)
    c = lax.broadcasted_iota(jnp.int32, (n2, n2), 1)
    same = (r < CHUNK) == (c < CHUNK)
    lower = same & (c <= r)
    strict = same & (c < r)
    eye = (r == c).astype(F32)
    lane = _lane_iota((1, LANES))
    sub = lane % HEAD_DIM

    pairs = range(N_HEADS // 2)
    cps = [(ci, p) for ci in range(chunks) for p in pairs]
    win = {cp: (slice(cp[0] * CHUNK, (cp[0] + 1) * CHUNK), slice(cp[1] * LANES, (cp[1] + 1) * LANES))
           for cp in cps}

    diff_l, diff_r, kb_s, q_s, kk2, rhs, k_dec, q_dec, last = {}, {}, {}, {}, {}, {}, {}, {}, {}
    for cp in cps:
        rows, cols = win[cp]
        q2, k2, v2, gc2, beta2 = [a[:, cols] for a in vals[cp[0]]]
        hi, lo, lo2 = [u.astype(F32) for u in _split3(gc2)]
        one = jnp.where((sub >= 3) & (sub < 6), 1.0, 0.0)
        la = jnp.where(sub == 0, hi, jnp.where(sub == 1, lo, jnp.where(sub == 2, lo2, one)))
        one = jnp.where(sub < 3, 1.0, 0.0)
        ra = jnp.where(sub == 3, -hi, jnp.where(sub == 4, -lo, jnp.where(sub == 5, -lo2, one))).astype(BF16)
        diff_l[cp] = _stack_heads(la, half).astype(BF16)
        diff_r[cp] = jnp.concatenate([ra, ra], axis=0)
        eg = jnp.exp(gc2)
        kb = k2 * beta2
        kk = k2.astype(BF16)
        kk2[cp] = jnp.concatenate([kk, kk], axis=0)
        kb_s[cp] = _stack_heads(kb, half).astype(BF16)
        q_s[cp] = _stack_heads(q2, half).astype(BF16)
        rhs[cp] = jnp.concatenate([_stack_heads(v2 * beta2, half), _stack_heads(kb * eg, half)],
                                  axis=1).astype(BF16)
        k_dec[cp] = (k2 * jnp.exp(gc2[CHUNK - 1:CHUNK, :] - gc2)).astype(BF16)
        q_dec[cp] = q2 * eg
        last[cp] = eg[CHUNK - 1:CHUNK, :]

    diff = {cp: _dot_nt(diff_l[cp], diff_r[cp]) for cp in cps}
    akk = {cp: _dot_nt(kb_s[cp], kk2[cp]) for cp in cps}
    aqk = {cp: _dot_nt(q_s[cp], kk2[cp]) for cp in cps}

    x, pw, intra = {}, {}, {}
    for cp in cps:
        decay = jnp.where(lower, jnp.exp(jnp.minimum(diff[cp], 0.0)), 0.0)
        a = jnp.where(strict, akk[cp] * decay, 0.0)
        intra[cp] = (aqk[cp] * decay).astype(BF16)
        x[cp] = eye - a
        pw[cp] = a

    for _ in range(5):
        for cp in cps:
            pb = pw[cp].astype(BF16)
            pw[cp] = _dot(pb, pb)
        for cp in cps:
            x[cp] = x[cp] + _dot(x[cp].astype(BF16), pw[cp].astype(BF16))

    uw = {cp: _dot(x[cp].astype(BF16), rhs[cp]) for cp in cps}
    u2 = {cp: _unstack(uw[cp][:, :LANES]) for cp in cps}
    w2 = {cp: _unstack(uw[cp][:, LANES:]) for cp in cps}

    uw_stack = {cp: jnp.concatenate([_stack_heads(u2[cp], half), _stack_heads(w2[cp], half)],
                                    axis=1).astype(BF16) for cp in cps}
    uw_flat = {cp: jnp.concatenate([u2[cp], w2[cp]], axis=1).astype(BF16) for cp in cps}
    iuw = {cp: _dot(intra[cp], uw_stack[cp]) for cp in cps}
    kuw = {cp: _dot_tn(k_dec[cp], uw_flat[cp]) for cp in cps}
    o_loc, lhs, s_add = {}, {}, {}
    for cp in cps:
        o_loc[cp] = _unstack(iuw[cp][:, :LANES])
        q_eff = q_dec[cp] - _unstack(iuw[cp][:, LANES:])
        m_eff = jnp.where(same, -kuw[cp][:, LANES:], 0.0)
        lhs[cp] = jnp.concatenate([q_eff, m_eff], axis=0).astype(BF16)
        s_add[cp] = jnp.where(same, kuw[cp][:, :LANES], 0.0)

    state = dict(state)
    for ci in range(chunks):
        res = {}
        for p in pairs:
            hi, lo = _split2(state[p])
            res[p] = _dot(lhs[(ci, p)], jnp.concatenate([hi, lo], axis=1))
        for p in pairs:
            cp = (ci, p)
            rr = res[p][:, :LANES] + res[p][:, LANES:]
            o2 = rr[:CHUNK, :] + o_loc[cp]
            state[p] = state[p] * last[cp] + rr[CHUNK:, :] + s_add[cp]
            rows, cols = win[cp]
            o_ref[rows, cols] = _pair_rms(o2, gain_ref[...], *half).astype(o_ref.dtype)
    return state


def _gdn(big, small, conv_w, alog_row, dtb_row, gain2, batch, seq, chunks=4):
    t = batch * seq
    tm = chunks * CHUNK
    nb = seq // tm
    ratio = tm // GDN_HALO
    grp = np.kron(np.eye(N_HEADS, dtype=np.float32), np.ones((HEAD_DIM, HEAD_DIM), np.float32))
    eg = np.zeros((LANES, BRANCH_WIDTH), np.float32)
    eb = np.zeros((LANES, BRANCH_WIDTH), np.float32)
    for h in range(N_HEADS):
        eg[L_GDNA + h, h * HEAD_DIM:(h + 1) * HEAD_DIM] = 1.0
        eb[L_GDNB + h, h * HEAD_DIM:(h + 1) * HEAD_DIM] = 1.0
    tri = np.kron(np.eye(chunks, dtype=np.float32), np.tril(np.ones((CHUNK, CHUNK), np.float32)))
    const = lambda shape: pl.BlockSpec(shape, lambda b, n: (0, 0))
    sec0 = C_GDN * LANES // BRANCH_WIDTH
    xs = [pl.BlockSpec((tm, BRANCH_WIDTH), functools.partial(lambda b, n, s: (b * nb + n, sec0 + s), s=s))
          for s in range(3)]
    halos = [pl.BlockSpec((GDN_HALO, BRANCH_WIDTH), functools.partial(
        lambda b, n, s: (jnp.maximum((b * nb + n) * ratio - 1, 0), sec0 + s), s=s)) for s in range(3)]
    return pl.pallas_call(
        functools.partial(_gdn_kernel, chunks=chunks),
        grid=(batch, nb),
        in_specs=xs + halos + [
            pl.BlockSpec((tm, LANES), lambda b, n: (b * nb + n, 0)),
            const((GDN_CONV, 3 * BRANCH_WIDTH)),
            const((1, LANES)),
            const((1, LANES)),
            const((BRANCH_WIDTH, BRANCH_WIDTH)),
            const((LANES, BRANCH_WIDTH)),
            const((LANES, BRANCH_WIDTH)),
            const((tm, tm)),
            const((1, LANES)),
        ],
        out_specs=pl.BlockSpec((tm, BRANCH_WIDTH), lambda b, n: (b * nb + n, 0)),
        out_shape=jax.ShapeDtypeStruct((t, BRANCH_WIDTH), BF16),
        scratch_shapes=[pltpu.VMEM((N_HEADS // 2, LANES, LANES), F32)],
        compiler_params=_params(("arbitrary", "arbitrary")),
        name="gdn",
    )(big, big, big, big, big, big, small, conv_w, alog_row, dtb_row, jnp.asarray(grp, BF16),
      jnp.asarray(eg, BF16), jnp.asarray(eb, BF16), jnp.asarray(tri, BF16), gain2)


GLA_LEVELS = 6


def _gla_constants():
    n = CHUNK
    t = np.arange(n)
    blocks = []
    masks = []
    for lev in range(GLA_LEVELS):
        w = 1 << lev
        lower = (t // w) % 2 == 1
        mid_q = (t // w) * w
        cq = (lower[:, None] & (t[None, :] > mid_q[:, None]) & (t[None, :] <= t[:, None])).astype(np.float32)
        mid_k = (t // w + 1) * w
        ck = ((~lower)[:, None] & (t[None, :] > t[:, None]) & (t[None, :] <= mid_k[:, None])).astype(np.float32)
        blocks.append(cq + ck)
        m = lower[:, None] & (~lower)[None, :] & ((t // (2 * w))[:, None] == (t // (2 * w))[None, :])
        masks.append(np.kron(np.eye(2), m.astype(np.float32)))
    masks.append(np.eye(2 * n, dtype=np.float32))
    tri = np.tril(np.ones((n, n), np.float32))
    rev = (t[None, :] > t[:, None]).astype(np.float32)
    blocks += [tri, rev]
    return np.concatenate(blocks, axis=0), np.stack(masks)


def _gla_kernel(x_ref, small_ref, w2_ref, bias_ref, sel_ref, mask_ref, gain_ref, o_ref, state_ref, *, chunks):
    @pl.when(pl.program_id(1) == 0)
    def _():
        state_ref[...] = jnp.zeros_like(state_ref)

    half = _half_masks()
    qk_w = GLA_HEADS * HEAD_DIM
    r = lax.broadcasted_iota(jnp.int32, (2 * LANES, LANES), 0)
    c = lax.broadcasted_iota(jnp.int32, (2 * LANES, LANES), 1)
    state_mask = (r < LANES) == (c < HEAD_DIM)
    zeros_v = jnp.zeros((CHUNK, LANES), F32)

    pairs = range(GLA_HEADS // 2)
    cps = [(ci, p) for ci in range(chunks) for p in pairs]
    crow = lambda ci: slice(ci * CHUNK, (ci + 1) * CHUNK)

    log_a = {}
    for ci in range(chunks):
        y = _dot(small_ref[crow(ci), :].astype(BF16), w2_ref[...]) + bias_ref[...]
        log_a[ci] = _log_sigmoid(y) * (1.0 / GLA_GATE_TAU)

    q2, k2, v_pair, la_split = {}, {}, {}, {}
    for ci, p in cps:
        rows = crow(ci)
        q2[ci, p] = x_ref[rows, p * LANES:(p + 1) * LANES].astype(F32) * (HEAD_DIM ** -0.5)
        k2[ci, p] = x_ref[rows, qk_w + p * LANES:qk_w + (p + 1) * LANES].astype(F32)
        v_pair[ci, p] = x_ref[rows, 2 * qk_w + 2 * p * LANES:2 * qk_w + 2 * (p + 1) * LANES]
        hi, lo = _split2(log_a[ci][:, p * LANES:(p + 1) * LANES])
        la_split[ci, p] = jnp.concatenate([hi, lo], axis=1)

    sums = {}
    for cp in cps:
        s2 = _dot(sel_ref[...], la_split[cp])
        sums[cp] = s2[:, :LANES] + s2[:, LANES:]

    a = {cp: None for cp in cps}
    for lev in range(GLA_LEVELS + 1):
        lhs, rhs = {}, {}
        for cp in cps:
            if lev < GLA_LEVELS:
                dec = jnp.exp(sums[cp][lev * CHUNK:(lev + 1) * CHUNK, :])
                ql, kl = q2[cp] * dec, k2[cp] * dec
            else:
                ql, kl = q2[cp], k2[cp]
            kb = kl.astype(BF16)
            lhs[cp] = _stack_heads(ql, half).astype(BF16)
            rhs[cp] = jnp.concatenate([kb, kb], axis=0)
        prod = {cp: _dot_nt(lhs[cp], rhs[cp]) for cp in cps}
        for cp in cps:
            term = prod[cp] * mask_ref[lev]
            a[cp] = term if a[cp] is None else a[cp] + term

    v_stack, q_dec, k_dec, b_last = {}, {}, {}, {}
    for cp in cps:
        b = sums[cp][GLA_LEVELS * CHUNK:(GLA_LEVELS + 1) * CHUNK, :]
        b_rev = sums[cp][(GLA_LEVELS + 1) * CHUNK:, :]
        vf = v_pair[cp].astype(F32)
        v_stack[cp] = jnp.concatenate(
            [jnp.concatenate([vf[:, :LANES], zeros_v], axis=1),
             jnp.concatenate([zeros_v, vf[:, LANES:]], axis=1)], axis=0).astype(BF16)
        q_dec[cp] = (q2[cp] * jnp.exp(b)).astype(BF16)
        k_dec[cp] = (k2[cp] * jnp.exp(b_rev)).astype(BF16)
        b_last[cp] = jnp.exp(b[CHUNK - 1:CHUNK, :])
    o_loc = {cp: _unstack(_dot(a[cp].astype(BF16), v_stack[cp])) for cp in cps}
    upd = {cp: jnp.where(state_mask, _dot_tn(v_pair[cp], k_dec[cp]), 0.0) for cp in cps}

    state = {p: state_ref[p] for p in pairs}
    for ci in range(chunks):
        o_pair = {p: o_loc[ci, p] + _dot_nt(q_dec[ci, p], state[p].astype(BF16)) for p in pairs}
        for p in pairs:
            state[p] = state[p] * b_last[ci, p] + upd[ci, p]
            for h in range(2):
                oh = o_pair[p][:, h * LANES:(h + 1) * LANES]
                ms = jnp.mean(oh * oh, axis=-1, keepdims=True)
                col0 = (2 * p + h) * LANES
                o_ref[crow(ci), col0:col0 + LANES] = (
                    oh * lax.rsqrt(ms + NORM_EPS) * gain_ref[...]).astype(o_ref.dtype)
    for p in pairs:
        state_ref[p] = state[p]


def _gla(big, small, w2_ext, bias_row, gain_row, batch, seq, chunks=8):
    t = batch * seq
    tm = chunks * CHUNK
    nb = seq // tm
    sel, masks = _gla_constants()
    width = 2 * GLA_HEADS * HEAD_DIM + BRANCH_WIDTH
    const2 = lambda shape: pl.BlockSpec(shape, lambda b, n: (0, 0))
    return pl.pallas_call(
        functools.partial(_gla_kernel, chunks=chunks),
        grid=(batch, nb),
        in_specs=[
            pl.BlockSpec((tm, width), lambda b, n: (b * nb + n, C_GLA * LANES // width)),
            pl.BlockSpec((tm, LANES), lambda b, n: (b * nb + n, 0)),
            const2((LANES, GLA_HEADS * HEAD_DIM)),
            const2((1, GLA_HEADS * HEAD_DIM)),
            const2(sel.shape),
            pl.BlockSpec(masks.shape, lambda b, n: (0, 0, 0)),
            const2((1, LANES)),
        ],
        out_specs=pl.BlockSpec((tm, BRANCH_WIDTH), lambda b, n: (b * nb + n, 0)),
        out_shape=jax.ShapeDtypeStruct((t, BRANCH_WIDTH), BF16),
        scratch_shapes=[pltpu.VMEM((GLA_HEADS // 2, 2 * LANES, LANES), F32)],
        compiler_params=_params(("arbitrary", "arbitrary")),
        name="gla",
    )(big, small, w2_ext, bias_row, jnp.asarray(sel, BF16), jnp.asarray(masks, F32), gain_row)


def _merge_kernel(x_ref, o0_ref, o1_ref, o2_ref, o3_ref, z_ref, ml_ref, gb_ref, wb_ref, wo_ref, out_ref):
    merged = None
    for n, o_ref in enumerate((o0_ref, o1_ref, o2_ref, o3_ref)):
        z = z_ref[:, n * BRANCH_WIDTH:(n + 1) * BRANCH_WIDTH].astype(F32)
        br = (o_ref[...].astype(F32) * _silu(z)).astype(BF16)
        proj = _dot(br, wb_ref[n])
        gate = 1.0 / (1.0 + jnp.exp2(ml_ref[:, n * D_MODEL:(n + 1) * D_MODEL].astype(F32)
                                     + gb_ref[:, n * D_MODEL:(n + 1) * D_MODEL]))
        term = proj * gate
        merged = term if merged is None else merged + term
    out_ref[...] = x_ref[...] + _dot(merged.astype(BF16), wo_ref[...])


def _merge(x2, outs, big, gate_bias_row, w_branch, w_out, tm=256):
    t = x2.shape[0]
    zw = N_BRANCH * BRANCH_WIDTH
    mw = N_BRANCH * D_MODEL
    o_spec = pl.BlockSpec((tm, BRANCH_WIDTH), lambda i: (i, 0))
    return pl.pallas_call(
        _merge_kernel,
        grid=(t // tm,),
        in_specs=[
            pl.BlockSpec((tm, D_MODEL), lambda i: (i, 0)),
            o_spec, o_spec, o_spec, o_spec,
            pl.BlockSpec((tm, zw), lambda i: (i, C_Z * LANES // zw)),
            pl.BlockSpec((tm, mw), lambda i: (i, C_MERGE * LANES // mw)),
            pl.BlockSpec((1, mw), lambda i: (0, 0)),
            pl.BlockSpec((N_BRANCH, BRANCH_WIDTH, D_MODEL), lambda i: (0, 0, 0), pipeline_mode=pl.Buffered(1)),
            pl.BlockSpec((D_MODEL, D_MODEL), lambda i: (0, 0), pipeline_mode=pl.Buffered(1)),
        ],
        out_specs=pl.BlockSpec((tm, D_MODEL), lambda i: (i, 0)),
        out_shape=jax.ShapeDtypeStruct((t, D_MODEL), F32),
        compiler_params=_params(("arbitrary",)),
        name="merge_out",
    )(x2, *outs, big, big, gate_bias_row, w_branch, w_out)


def _reorder_w_in(w_in):
    sizes = (3 * BRANCH_WIDTH, N_HEADS, BRANCH_WIDTH, 3 * BRANCH_WIDTH, BRANCH_WIDTH,
             3 * BRANCH_WIDTH, N_HEADS, N_HEADS, BRANCH_WIDTH,
             2 * GLA_HEADS * HEAD_DIM, BRANCH_WIDTH, GLA_GATE_RANK, BRANCH_WIDTH, N_BRANCH * D_MODEL)
    (fox_qkv, fox_f, fox_z, sb_qkv, sb_z, gdn_qkv, gdn_a, gdn_b, gdn_z,
     gla_qk, gla_v, gla_g, gla_z, merge) = jnp.split(w_in, np.cumsum(sizes)[:-1].tolist(), axis=1)
    big = jnp.concatenate([merge * NEG_LOG2E, fox_z, sb_z, gdn_z, gla_z, gla_qk, gla_v, gdn_qkv, fox_qkv, sb_qkv],
                          axis=1)
    pad = jnp.zeros((w_in.shape[0], LANES - (3 * N_HEADS + GLA_GATE_RANK)), w_in.dtype)
    small = jnp.concatenate([fox_f, gdn_a, gdn_b, gla_g, pad], axis=1)
    return big.astype(BF16), small.astype(BF16)


def _lane_row(values, offset):
    row = jnp.zeros((1, LANES), F32)
    return row.at[0, offset:offset + values.shape[0]].set(values.astype(F32))


def _layer(x2, batch, seq, norm_g, w_in, fox_f_bias, fox_q_norm, fox_k_norm, sb_q_norm, sb_k_norm,
           gdn_conv_w, gdn_a_log, gdn_dt_bias, gdn_o_norm, gla_gate_w2, gla_gate_bias, gla_o_norm,
           gate_bias, w_branch, w_out):
    w_big, w_small = _reorder_w_in(w_in)
    g_row = norm_g.reshape(1, D_MODEL).astype(F32)
    t = batch * seq
    big, small = _norm_proj(x2, g_row, w_big, w_small, min(2048, t), 512)

    pair = lambda g: jnp.tile(g.astype(F32), 2).reshape(1, LANES)

    c = _fox_cum(small, _lane_row(fox_f_bias, L_FOXF), batch, seq)
    tq = tk = min(512, seq)
    c3 = c.reshape(batch, seq, LANES)[:, :, :N_HEADS]
    first = jnp.transpose(c3[:, 0::tq, :], (0, 2, 1))
    last = jnp.transpose(c3[:, tk - 1::tk, :], (0, 2, 1))
    logit_bound = lambda gq, gk: 1.02 * HEAD_DIM ** 0.5 * jnp.max(jnp.abs(gq)) * jnp.max(jnp.abs(gk))
    qk_bound = logit_bound(fox_q_norm, fox_k_norm)
    thr = EXP_UNDERFLOW + 2.0 * qk_bound + 1.0
    tab = jnp.concatenate([first.reshape(-1), last.reshape(-1), thr.reshape(1), qk_bound.reshape(1)]).astype(F32)
    o_fox = _fox_attention(big, c, tab, pair(fox_q_norm), pair(fox_k_norm), batch, seq, tq, tk)

    o_sb = _sb_attention(big, logit_bound(sb_q_norm, sb_k_norm).reshape(1).astype(F32),
                         pair(sb_q_norm), pair(sb_k_norm), batch, seq, min(256, seq))

    o_gdn = _gdn(big, small, gdn_conv_w.astype(F32), _lane_row(gdn_a_log, L_GDNA),
                 _lane_row(gdn_dt_bias, L_GDNA), pair(gdn_o_norm), batch, seq)

    w2_ext = jnp.zeros((LANES, GLA_HEADS * HEAD_DIM), F32).at[L_GLAG:L_GLAG + GLA_GATE_RANK].set(
        gla_gate_w2.astype(F32)).astype(BF16)
    o_gla = _gla(big, small, w2_ext, gla_gate_bias.reshape(1, -1).astype(F32),
                 gla_o_norm.reshape(1, LANES).astype(F32), batch, seq)

    return _merge(x2, (o_fox, o_sb, o_gdn, o_gla), big, gate_bias.reshape(1, -1).astype(F32) * NEG_LOG2E,
                  w_branch.astype(BF16), w_out.astype(BF16), min(512, t))


def kernel(x, norm_g, w_in, fox_f_bias, fox_q_norm, fox_k_norm, sb_q_norm, sb_k_norm, gdn_conv_w,
           gdn_a_log, gdn_dt_bias, gdn_o_norm, gla_gate_w2, gla_gate_bias, gla_o_norm, gate_bias,
           w_branch, w_out):
    batch, seq, d = x.shape
    x2 = x.reshape(batch * seq, d)
    params = (norm_g, w_in, fox_f_bias, fox_q_norm, fox_k_norm, sb_q_norm, sb_k_norm, gdn_conv_w,
              gdn_a_log, gdn_dt_bias, gdn_o_norm, gla_gate_w2, gla_gate_bias, gla_o_norm, gate_bias,
              w_branch, w_out)
    for l in range(norm_g.shape[0]):
        x2 = _layer(x2, batch, seq, *[p[l] for p in params])
    return x2.reshape(batch, seq, d)
```

```python
import functools

import numpy as np
import jax
import jax.numpy as jnp
from jax import lax
from jax.experimental import pallas as pl
from jax.experimental.pallas import tpu as pltpu

F32 = jnp.float32
BF16 = jnp.bfloat16

LANES = 128
HEAD_DIM = 64
D_MODEL = 1024
BRANCH_WIDTH = 512
N_BRANCH = 4
N_HEADS = 8
GLA_HEADS = 4
GLA_GATE_RANK = 16
GLA_GATE_TAU = 16.0
GDN_CONV = 4
CHUNK = 64
NORM_EPS = 1e-6
VMEM_LIMIT = 48 * 1024 * 1024
EXP_UNDERFLOW = 104.0
NEG_BIG = -1e30
NEG_LOG2E = -1.4426950408889634
STATIC_LOGIT_LIMIT = 20.0

C_MERGE, C_Z, C_GLA, C_GDN, C_FOX, C_SB, C_END = 0, 32, 48, 56, 68, 80, 92
N_BIG = C_END * LANES
L_FOXF, L_GDNA, L_GDNB, L_GLAG = 0, 8, 16, 24


def _dot(a, b):
    return jnp.dot(a, b, preferred_element_type=F32)


def _dot_nt(a, b):
    return lax.dot_general(a, b, (((1,), (1,)), ((), ())), preferred_element_type=F32)


def _dot_tn(a, b):
    return lax.dot_general(a, b, (((0,), (0,)), ((), ())), preferred_element_type=F32)


def _split2(x):
    hi = x.astype(BF16)
    lo = (x - hi.astype(F32)).astype(BF16)
    return hi, lo


def _split3(x):
    hi = x.astype(BF16)
    r = x - hi.astype(F32)
    lo = r.astype(BF16)
    lo2 = (r - lo.astype(F32)).astype(BF16)
    return hi, lo, lo2


def _sel_dot(sel, x):
    hi, lo, lo2 = _split3(x)
    return _dot(sel, hi) + _dot(sel, lo) + _dot(sel, lo2)


def _dot_sel(x, sel):
    hi, lo, lo2 = _split3(x)
    return _dot(hi, sel) + _dot(lo, sel) + _dot(lo2, sel)


def _softplus(x):
    return jnp.maximum(x, 0.0) + jnp.log(1.0 + jnp.exp(-jnp.abs(x)))


def _log_sigmoid(x):
    return jnp.minimum(x, 0.0) - jnp.log(1.0 + jnp.exp(-jnp.abs(x)))


def _sigmoid(x):
    return 1.0 / (1.0 + jnp.exp(-x))


def _silu(x):
    return x / (1.0 + jnp.exp(-x))


def _lane_iota(shape):
    return lax.broadcasted_iota(jnp.int32, shape, len(shape) - 1)


def _half_masks():
    lane = _lane_iota((1, LANES))
    return lane < HEAD_DIM, lane >= HEAD_DIM


def _params(semantics):
    return pltpu.CompilerParams(dimension_semantics=semantics, vmem_limit_bytes=VMEM_LIMIT)


def _proj_kernel(x_ref, g_ref, w_ref, ws_ref, o_ref, os_ref, h_ref):
    @pl.when(pl.program_id(1) == 0)
    def _():
        x = x_ref[...]
        ms = jnp.mean(x * x, axis=-1, keepdims=True)
        h_ref[...] = (x * lax.rsqrt(ms + NORM_EPS) * g_ref[...]).astype(BF16)
        os_ref[...] = _dot(h_ref[...], ws_ref[...])

    o_ref[...] = _dot(h_ref[...], w_ref[...]).astype(o_ref.dtype)


def _norm_proj(x2, g, w, w_small, tm, tn):
    t, d = x2.shape
    n = w.shape[1]
    return pl.pallas_call(
        _proj_kernel,
        grid=(t // tm, n // tn),
        in_specs=[
            pl.BlockSpec((tm, d), lambda i, j: (i, 0)),
            pl.BlockSpec((1, d), lambda i, j: (0, 0)),
            pl.BlockSpec((d, tn), lambda i, j: (0, j)),
            pl.BlockSpec((d, LANES), lambda i, j: (0, 0)),
        ],
        out_specs=[pl.BlockSpec((tm, tn), lambda i, j: (i, j)),
                   pl.BlockSpec((tm, LANES), lambda i, j: (i, 0))],
        out_shape=[jax.ShapeDtypeStruct((t, n), BF16), jax.ShapeDtypeStruct((t, LANES), F32)],
        scratch_shapes=[pltpu.VMEM((tm, d), BF16)],
        compiler_params=_params(("arbitrary", "arbitrary")),
        name="norm_proj",
    )(x2, g, w, w_small)


def _fox_cum_kernel(f_ref, bias_ref, tri_ref, c_ref, carry_ref):
    @pl.when(pl.program_id(1) == 0)
    def _():
        carry_ref[...] = jnp.zeros_like(carry_ref)

    lf = _log_sigmoid(f_ref[...] + bias_ref[...])
    cs = _sel_dot(tri_ref[...], lf) + carry_ref[...]
    c_ref[...] = cs
    carry_ref[...] = cs[-1:, :]


def _fox_cum(small, bias_row, batch, seq, tr=512):
    tri = jnp.asarray(np.tril(np.ones((tr, tr), np.float32)), BF16)
    nb = seq // tr
    return pl.pallas_call(
        _fox_cum_kernel,
        grid=(batch, nb),
        in_specs=[
            pl.BlockSpec((tr, LANES), lambda b, r: (b * nb + r, 0)),
            pl.BlockSpec((1, LANES), lambda b, r: (0, 0)),
            pl.BlockSpec((tr, tr), lambda b, r: (0, 0)),
        ],
        out_specs=pl.BlockSpec((tr, LANES), lambda b, r: (b * nb + r, 0)),
        out_shape=jax.ShapeDtypeStruct(small.shape, F32),
        scratch_shapes=[pltpu.VMEM((1, LANES), F32)],
        compiler_params=_params(("arbitrary", "arbitrary")),
        name="fox_cum",
    )(small, bias_row, tri)


def _pair_rms(x, gain, half0, half1):
    sq = x * x
    ms0 = jnp.sum(jnp.where(half0, sq, 0.0), axis=-1, keepdims=True) * (1.0 / HEAD_DIM)
    ms1 = jnp.sum(jnp.where(half1, sq, 0.0), axis=-1, keepdims=True) * (1.0 / HEAD_DIM)
    r = jnp.where(half0, lax.rsqrt(ms0 + NORM_EPS), lax.rsqrt(ms1 + NORM_EPS))
    return x * r * gain


def _lane_pick(x, idx):
    lane = _lane_iota((1, LANES))
    return jnp.sum(jnp.where(lane == idx, x, 0.0), axis=-1, keepdims=True)


def _aug_lanes(c_col, base, key_side):
    lane = _lane_iota((1, LANES))
    hi, lo, lo2 = [v.astype(F32) for v in _split3(c_col)]
    if key_side:
        hi, lo, lo2 = -hi, -lo, -lo2
        vals, ones = (base + 3, base + 4, base + 5), (base, base + 1, base + 2)
    else:
        vals, ones = (base, base + 1, base + 2), (base + 3, base + 4, base + 5)
    out = jnp.where((lane == ones[0]) | (lane == ones[1]) | (lane == ones[2]), 1.0, 0.0)
    out = jnp.where(lane == vals[0], hi, out)
    out = jnp.where(lane == vals[1], lo, out)
    out = jnp.where(lane == vals[2], lo2, out)
    return out


def _fox_kernel(tab_ref, q_ref, k_ref, v_ref, cq_ref, ck_ref, gq_ref, gk_ref, o_ref,
                kaug_ref, vaug_ref, acc_ref, m_ref, l_ref, *, seq, tq, tk, rows_per_pass):
    b, p, i = pl.program_id(0), pl.program_id(1), pl.program_id(2)
    nq, nk = seq // tq, seq // tk
    nb = pl.num_programs(0)
    half = _half_masks()
    lane = _lane_iota((1, LANES))
    scale = HEAD_DIM ** -0.5
    spare = [HEAD_DIM * (1 - h) for h in range(2)]

    @pl.when(i == 0)
    def _():
        def fill(r, carry):
            rows = pl.ds(pl.multiple_of(r * rows_per_pass, rows_per_pass), rows_per_pass)
            kn = _pair_rms(k_ref[rows, :].astype(F32), gk_ref[...], *half)
            cc = ck_ref[rows, :]
            vv = v_ref[rows, :]
            for h in range(2):
                aug = _aug_lanes(_lane_pick(cc, 2 * p + h), spare[h], True)
                kaug_ref[h, rows, :] = jnp.where(half[h], kn, aug).astype(BF16)
                ones = jnp.where(lane == spare[h], 1.0, 0.0).astype(BF16)
                vaug_ref[h, rows, :] = jnp.where(half[h], vv, ones)
            return carry
        lax.fori_loop(0, seq // rows_per_pass, fill, 0)

    first_off = N_HEADS * nb * nq
    thr = tab_ref[first_off + N_HEADS * nb * nk]
    bound = tab_ref[first_off + N_HEADS * nb * nk + 1]
    fast = bound < STATIC_LOGIT_LIMIT

    qn = _pair_rms(q_ref[...].astype(F32), gq_ref[...], *half) * scale
    cq = cq_ref[...]
    tile_i = i // (tk // tq)
    row = lax.broadcasted_iota(jnp.int32, (tq, tk), 0) + (i % (tk // tq)) * tq
    col = lax.broadcasted_iota(jnp.int32, (tq, tk), 1)
    causal = col <= row

    c_first = [tab_ref[(b * N_HEADS + 2 * p + h) * nq + i] for h in range(2)]

    def cond(j):
        jj = jnp.maximum(j, 0)
        need = [c_first[h] - tab_ref[first_off + (b * N_HEADS + 2 * p + h) * nk + jj] >= -thr for h in range(2)]
        return jnp.logical_and(j >= 0, jnp.logical_or(need[0], need[1]))

    def walk(tile):
        tile(tile_i, True)

        def body(j):
            tile(j, False)
            return j - 1
        lax.while_loop(cond, body, tile_i - 1)

    def q_aug(shift):
        return [jnp.where(half[h], qn, _aug_lanes(_lane_pick(cq, 2 * p + h) - shift, spare[h], False)).astype(BF16)
                for h in range(2)]

    @pl.when(fast)
    def _():
        qa = q_aug(bound)
        acc_ref[...] = jnp.zeros_like(acc_ref)

        def tile(j, masked):
            rows = pl.ds(pl.multiple_of(j * tk, tk), tk)
            s = [_dot_nt(qa[h], kaug_ref[h, rows, :]) for h in range(2)]
            if masked:
                s = [jnp.where(causal, sh, NEG_BIG) for sh in s]
            pr = [jnp.exp(sh).astype(BF16) for sh in s]
            for h in range(2):
                acc_ref[h] = acc_ref[h] + _dot(pr[h], vaug_ref[h, rows, :])

        walk(tile)
        out = [acc_ref[h] / acc_ref[h][:, spare[h]:spare[h] + 1] for h in range(2)]
        o_ref[...] = jnp.where(half[0], out[0], out[1]).astype(o_ref.dtype)

    @pl.when(jnp.logical_not(fast))
    def _():
        qa = q_aug(0.0)
        acc_ref[...] = jnp.zeros_like(acc_ref)
        m_ref[...] = jnp.full_like(m_ref, NEG_BIG)
        l_ref[...] = jnp.zeros_like(l_ref)

        def tile(j, masked):
            rows = pl.ds(pl.multiple_of(j * tk, tk), tk)
            for h in range(2):
                s = _dot_nt(qa[h], kaug_ref[h, rows, :])
                if masked:
                    s = jnp.where(causal, s, NEG_BIG)
                m_prev = m_ref[h]
                m_new = jnp.maximum(m_prev, jnp.max(s, axis=1, keepdims=True))
                alpha = jnp.exp(m_prev - m_new)
                pr = jnp.exp(s - m_new)
                l_ref[h] = alpha * l_ref[h] + jnp.sum(pr, axis=1, keepdims=True)
                acc_ref[h] = alpha * acc_ref[h] + _dot(pr.astype(BF16), vaug_ref[h, rows, :])
                m_ref[h] = m_new

        walk(tile)
        out = [acc_ref[h] / l_ref[h] for h in range(2)]
        o_ref[...] = jnp.where(half[0], out[0], out[1]).astype(o_ref.dtype)


def _fox_attention(big, c, tab, gq2, gk2, batch, seq, tq, tk):
    nq = seq // tq
    t = batch * seq
    kern = functools.partial(_fox_kernel, seq=seq, tq=tq, tk=tk, rows_per_pass=min(512, seq))
    grid_spec = pltpu.PrefetchScalarGridSpec(
        num_scalar_prefetch=1,
        grid=(batch, N_HEADS // 2, nq),
        in_specs=[
            pl.BlockSpec((tq, LANES), lambda b, p, i, tab: (b * nq + i, C_FOX + p)),
            pl.BlockSpec((seq, LANES), lambda b, p, i, tab: (b, C_FOX + 4 + p)),
            pl.BlockSpec((seq, LANES), lambda b, p, i, tab: (b, C_FOX + 8 + p)),
            pl.BlockSpec((tq, LANES), lambda b, p, i, tab: (b * nq + i, 0)),
            pl.BlockSpec((seq, LANES), lambda b, p, i, tab: (b, 0)),
            pl.BlockSpec((1, LANES), lambda b, p, i, tab: (0, 0)),
            pl.BlockSpec((1, LANES), lambda b, p, i, tab: (0, 0)),
        ],
        out_specs=pl.BlockSpec((tq, LANES), lambda b, p, i, tab: (b * nq + i, p)),
        scratch_shapes=[
            pltpu.VMEM((2, seq, LANES), BF16),
            pltpu.VMEM((2, seq, LANES), BF16),
            pltpu.VMEM((2, tq, LANES), F32),
            pltpu.VMEM((2, tq, 1), F32),
            pltpu.VMEM((2, tq, 1), F32),
        ],
    )
    return pl.pallas_call(
        kern,
        grid_spec=grid_spec,
        out_shape=jax.ShapeDtypeStruct((t, BRANCH_WIDTH), BF16),
        compiler_params=_params(("arbitrary", "arbitrary", "arbitrary")),
        name="fox_attention",
    )(tab, big, big, big, c, c, gq2, gk2)


SB_MASKED_LOGIT = -1e4


def _sb_kernel(bound_ref, q_ref, k_ref, v_ref, gq_ref, gk_ref, lower_ref, o_ref,
               kn_ref, acc_ref, run_ref, flag_ref, *, seq, tq, nsub, rows_per_pass):
    i = pl.program_id(2)
    half = _half_masks()
    scale = HEAD_DIM ** -0.5

    @pl.when(i == 0)
    def _():
        def fill(r, carry):
            rows = pl.ds(pl.multiple_of(r * rows_per_pass, rows_per_pass), rows_per_pass)
            kn_ref[rows, :] = _pair_rms(k_ref[rows, :].astype(F32), gk_ref[...], *half).astype(BF16)
            return carry
        lax.fori_loop(0, seq // rows_per_pass, fill, 0)

    qn = _pair_rms(q_ref[...].astype(F32), gq_ref[...], *half) * scale
    qh = [[jnp.where(half[h], qn[s * tq:(s + 1) * tq], 0.0).astype(BF16) for h in range(2)] for s in range(nsub)]
    acc_ref[...] = jnp.zeros_like(acc_ref)
    run_ref[...] = jnp.zeros_like(run_ref)
    block = [i * nsub + s for s in range(nsub)]
    first_block = [jnp.maximum(blk - 1, 0) for blk in block]

    def step(tiles, bounded):
        kk, vv, low, z = {}, {}, {}, {}
        for s, start, width, offset in tiles:
            rows = pl.ds(pl.multiple_of(start, tq), width)
            kk[s], vv[s], low[s] = kn_ref[rows, :], v_ref[rows, :], lower_ref[:width, :width]
        for s, start, width, offset in tiles:
            for h in range(2):
                z[s, h] = _dot_nt(qh[s][h], kk[s])
        for s, start, width, offset in tiles:
            if offset is not None:
                row = lax.broadcasted_iota(jnp.int32, (tq, width), 0) + offset
                col = lax.broadcasted_iota(jnp.int32, (tq, width), 1)
                for h in range(2):
                    z[s, h] = jnp.where(col < row, z[s, h], SB_MASKED_LOGIT)
        keys = list(z)
        if bounded:
            log_rest = {k: -jnp.log(1.0 + jnp.exp(z[k])) for k in keys}
        else:
            log_rest = {k: -_softplus(z[k]) for k in keys}
        inner = {k: _dot(log_rest[k].astype(BF16), low[k[0]]) for k in keys}
        w = {k: jnp.exp((z[k] + run_ref[k[0], k[1]]) + inner[k]).astype(BF16) for k in keys}
        for k in keys:
            acc_ref[k[0], k[1]] = acc_ref[k[0], k[1]] + _dot(w[k], vv[k[0]])
        run = {k: run_ref[k[0], k[1]] + jnp.sum(log_rest[k], axis=1, keepdims=True) for k in keys}
        for k in keys:
            run_ref[k[0], k[1]] = run[k]
        for s, start, width, offset in tiles:
            flag_ref[s] = jnp.maximum(jnp.max(run[s, 0]), jnp.max(run[s, 1]))

    def walk(bounded):
        step([(s, first_block[s] * tq, 2 * tq, (block[s] - first_block[s]) * tq) for s in range(nsub)], bounded)
        for s in range(nsub):
            def cond(j, s=s):
                return jnp.logical_and(j >= 0, flag_ref[s] > -EXP_UNDERFLOW)

            def body(j, s=s):
                step([(s, j * tq, tq, None)], bounded)
                return j - 1

            lax.while_loop(cond, body, first_block[s] - 1)
            o_ref[s * tq:(s + 1) * tq, :] = jnp.where(half[0], acc_ref[s, 0], acc_ref[s, 1]).astype(o_ref.dtype)

    bounded = bound_ref[0] < STATIC_LOGIT_LIMIT
    pl.when(bounded)(lambda: walk(True))
    pl.when(jnp.logical_not(bounded))(lambda: walk(False))


def _sb_attention(big, bound, gq2, gk2, batch, seq, tq=256):
    nsub = 2 if seq % (2 * tq) == 0 else 1
    tb = nsub * tq
    nq = seq // tb
    t = batch * seq
    lower = jnp.asarray(np.tril(np.ones((2 * tq, 2 * tq), np.float32)), BF16)
    kern = functools.partial(_sb_kernel, seq=seq, tq=tq, nsub=nsub, rows_per_pass=min(512, seq))
    return pl.pallas_call(
        kern,
        grid=(batch, N_HEADS // 2, nq),
        in_specs=[
            pl.BlockSpec(memory_space=pltpu.SMEM),
            pl.BlockSpec((tb, LANES), lambda b, p, i: (b * nq + i, C_SB + p)),
            pl.BlockSpec((seq, LANES), lambda b, p, i: (b, C_SB + 4 + p)),
            pl.BlockSpec((seq, LANES), lambda b, p, i: (b, C_SB + 8 + p)),
            pl.BlockSpec((1, LANES), lambda b, p, i: (0, 0)),
            pl.BlockSpec((1, LANES), lambda b, p, i: (0, 0)),
            pl.BlockSpec((2 * tq, 2 * tq), lambda b, p, i: (0, 0)),
        ],
        out_specs=pl.BlockSpec((tb, LANES), lambda b, p, i: (b * nq + i, p)),
        out_shape=jax.ShapeDtypeStruct((t, BRANCH_WIDTH), BF16),
        scratch_shapes=[
            pltpu.VMEM((seq, LANES), BF16),
            pltpu.VMEM((nsub, 2, tq, LANES), F32),
            pltpu.VMEM((nsub, 2, tq, 1), F32),
            pltpu.SMEM((nsub,), F32),
        ],
        compiler_params=_params(("arbitrary", "arbitrary", "arbitrary")),
        name="sb_attention",
    )(bound, big, big, big, gq2, gk2, lower)


GDN_HALO = 16


def _gdn_prep_vpu(first, xq_ref, xk_ref, xv_ref, hq_ref, hk_ref, hv_ref, small_ref, cw_ref, alog_ref, dtb_ref,
                  grp_ref, eg_ref, eb_ref, tri_ref):
    def conv_silu(x_ref, halo_ref, sec):
        halo = jnp.where(first, 0.0, halo_ref[...].astype(F32))
        xx = jnp.concatenate([halo, x_ref[...].astype(F32)], axis=0)
        cols = slice(sec * BRANCH_WIDTH, (sec + 1) * BRANCH_WIDTH)
        y = cw_ref[GDN_CONV - 1:GDN_CONV, cols] * xx[GDN_HALO:, :]
        for d in range(1, GDN_CONV):
            y = y + cw_ref[GDN_CONV - 1 - d:GDN_CONV - d, cols] * pltpu.roll(xx, d, 0)[GDN_HALO:, :]
        return _silu(y)

    q, k, v = conv_silu(xq_ref, hq_ref, 0), conv_silu(xk_ref, hk_ref, 1), conv_silu(xv_ref, hv_ref, 2)
    sm = small_ref[...]
    g = -jnp.exp(alog_ref[...]) * _softplus(sm + dtb_ref[...])
    return dict(q=q, k=k, v=v, q_sq=_split2(q * q), k_sq=_split2(k * k), g=g, beta=_split2(_sigmoid(sm)))


def _gdn_prep_mxu(part, grp_ref, eg_ref, eb_ref, tri_ref):
    def l2(u, sq):
        ss = _dot(sq[0], grp_ref[...]) + _dot(sq[1], grp_ref[...])
        return u * lax.rsqrt(ss + NORM_EPS)

    q = l2(part["q"], part["q_sq"]) * (HEAD_DIM ** -0.5)
    k = l2(part["k"], part["k_sq"])
    gc = _dot_sel(_sel_dot(tri_ref[...], part["g"]), eg_ref[...])
    beta = sum(_dot(term, eb_ref[...]) for term in part["beta"])
    return q, k, part["v"], gc, beta


def _stack_heads(x, half):
    return jnp.concatenate([jnp.where(half[0], x, 0.0), jnp.where(half[1], x, 0.0)], axis=0)


def _unstack(x):
    return x[:CHUNK, :] + x[CHUNK:, :]


GDN_PREP_INPUTS = 7
GDN_PREP_CONSTS = 7


def _gdn_kernel(*refs, chunks):
    tile = refs[:GDN_PREP_INPUTS]
    consts = refs[GDN_PREP_INPUTS:GDN_PREP_INPUTS + GDN_PREP_CONSTS]
    gain_ref, o_ref, state_ref = refs[GDN_PREP_INPUTS + GDN_PREP_CONSTS:]
    pairs = range(N_HEADS // 2)
    first = pl.program_id(1) == 0

    @pl.when(first)
    def _():
        state_ref[...] = jnp.zeros_like(state_ref)

    whole = _gdn_prep_mxu(_gdn_prep_vpu(first, *tile, *consts), *consts[3:])
    vals = [[a[c * CHUNK:(c + 1) * CHUNK, :] for a in whole] for c in range(chunks)]
    state = _gdn_scan_tile(vals, {p: state_ref[p] for p in pairs}, gain_ref, o_ref, chunks)
    for p in pairs:
        state_ref[p] = state[p]


def _gdn_scan_tile(vals, state, gain_ref, o_ref, chunks):
    half = _half_masks()
    n2 = 2 * CHUNK
    r = lax.broadcasted_iota(jnp.int32, (n2, n2), 0)
    c = lax.broadcasted_iota(jnp.int32, (n2, n2), 1)
    same = (r < CHUNK) == (c < CHUNK)
    lower = same & (c <= r)
    strict = same & (c < r)
    eye = (r == c).astype(F32)
    lane = _lane_iota((1, LANES))
    sub = lane % HEAD_DIM

    pairs = range(N_HEADS // 2)
    cps = [(ci, p) for ci in range(chunks) for p in pairs]
    win = {cp: (slice(cp[0] * CHUNK, (cp[0] + 1) * CHUNK), slice(cp[1] * LANES, (cp[1] + 1) * LANES))
           for cp in cps}

    diff_l, diff_r, kb_s, q_s, kk2, rhs, k_dec, q_dec, last = {}, {}, {}, {}, {}, {}, {}, {}, {}
    for cp in cps:
        rows, cols = win[cp]
        q2, k2, v2, gc2, beta2 = [a[:, cols] for a in vals[cp[0]]]
        hi, lo, lo2 = [u.astype(F32) for u in _split3(gc2)]
        one = jnp.where((sub >= 3) & (sub < 6), 1.0, 0.0)
        la = jnp.where(sub == 0, hi, jnp.where(sub == 1, lo, jnp.where(sub == 2, lo2, one)))
        one = jnp.where(sub < 3, 1.0, 0.0)
        ra = jnp.where(sub == 3, -hi, jnp.where(sub == 4, -lo, jnp.where(sub == 5, -lo2, one))).astype(BF16)
        diff_l[cp] = _stack_heads(la, half).astype(BF16)
        diff_r[cp] = jnp.concatenate([ra, ra], axis=0)
        eg = jnp.exp(gc2)
        kb = k2 * beta2
        kk = k2.astype(BF16)
        kk2[cp] = jnp.concatenate([kk, kk], axis=0)
        kb_s[cp] = _stack_heads(kb, half).astype(BF16)
        q_s[cp] = _stack_heads(q2, half).astype(BF16)
        rhs[cp] = jnp.concatenate([_stack_heads(v2 * beta2, half), _stack_heads(kb * eg, half)],
                                  axis=1).astype(BF16)
        k_dec[cp] = (k2 * jnp.exp(gc2[CHUNK - 1:CHUNK, :] - gc2)).astype(BF16)
        q_dec[cp] = q2 * eg
        last[cp] = eg[CHUNK - 1:CHUNK, :]

    diff = {cp: _dot_nt(diff_l[cp], diff_r[cp]) for cp in cps}
    akk = {cp: _dot_nt(kb_s[cp], kk2[cp]) for cp in cps}
    aqk = {cp: _dot_nt(q_s[cp], kk2[cp]) for cp in cps}

    x, pw, intra = {}, {}, {}
    for cp in cps:
        decay = jnp.where(lower, jnp.exp(jnp.minimum(diff[cp], 0.0)), 0.0)
        a = jnp.where(strict, akk[cp] * decay, 0.0)
        intra[cp] = (aqk[cp] * decay).astype(BF16)
        x[cp] = eye - a
        pw[cp] = a

    for _ in range(5):
        for cp in cps:
            pb = pw[cp].astype(BF16)
            pw[cp] = _dot(pb, pb)
        for cp in cps:
            x[cp] = x[cp] + _dot(x[cp].astype(BF16), pw[cp].astype(BF16))

    uw = {cp: _dot(x[cp].astype(BF16), rhs[cp]) for cp in cps}
    u2 = {cp: _unstack(uw[cp][:, :LANES]) for cp in cps}
    w2 = {cp: _unstack(uw[cp][:, LANES:]) for cp in cps}

    uw_stack = {cp: jnp.concatenate([_stack_heads(u2[cp], half), _stack_heads(w2[cp], half)],
                                    axis=1).astype(BF16) for cp in cps}
    uw_flat = {cp: jnp.concatenate([u2[cp], w2[cp]], axis=1).astype(BF16) for cp in cps}
    iuw = {cp: _dot(intra[cp], uw_stack[cp]) for cp in cps}
    kuw = {cp: _dot_tn(k_dec[cp], uw_flat[cp]) for cp in cps}
    o_loc, lhs, s_add = {}, {}, {}
    for cp in cps:
        o_loc[cp] = _unstack(iuw[cp][:, :LANES])
        q_eff = q_dec[cp] - _unstack(iuw[cp][:, LANES:])
        m_eff = jnp.where(same, -kuw[cp][:, LANES:], 0.0)
        lhs[cp] = jnp.concatenate([q_eff, m_eff], axis=0).astype(BF16)
        s_add[cp] = jnp.where(same, kuw[cp][:, :LANES], 0.0)

    state = dict(state)
    for ci in range(chunks):
        res = {}
        for p in pairs:
            hi, lo = _split2(state[p])
            res[p] = _dot(lhs[(ci, p)], jnp.concatenate([hi, lo], axis=1))
        for p in pairs:
            cp = (ci, p)
            rr = res[p][:, :LANES] + res[p][:, LANES:]
            o2 = rr[:CHUNK, :] + o_loc[cp]
            state[p] = state[p] * last[cp] + rr[CHUNK:, :] + s_add[cp]
            rows, cols = win[cp]
            o_ref[rows, cols] = _pair_rms(o2, gain_ref[...], *half).astype(o_ref.dtype)
    return state


def _gdn(big, small, conv_w, alog_row, dtb_row, gain2, batch, seq, chunks=4):
    t = batch * seq
    tm = chunks * CHUNK
    nb = seq // tm
    ratio = tm // GDN_HALO
    grp = np.kron(np.eye(N_HEADS, dtype=np.float32), np.ones((HEAD_DIM, HEAD_DIM), np.float32))
    eg = np.zeros((LANES, BRANCH_WIDTH), np.float32)
    eb = np.zeros((LANES, BRANCH_WIDTH), np.float32)
    for h in range(N_HEADS):
        eg[L_GDNA + h, h * HEAD_DIM:(h + 1) * HEAD_DIM] = 1.0
        eb[L_GDNB + h, h * HEAD_DIM:(h + 1) * HEAD_DIM] = 1.0
    tri = np.kron(np.eye(chunks, dtype=np.float32), np.tril(np.ones((CHUNK, CHUNK), np.float32)))
    const = lambda shape: pl.BlockSpec(shape, lambda b, n: (0, 0))
    sec0 = C_GDN * LANES // BRANCH_WIDTH
    xs = [pl.BlockSpec((tm, BRANCH_WIDTH), functools.partial(lambda b, n, s: (b * nb + n, sec0 + s), s=s))
          for s in range(3)]
    halos = [pl.BlockSpec((GDN_HALO, BRANCH_WIDTH), functools.partial(
        lambda b, n, s: (jnp.maximum((b * nb + n) * ratio - 1, 0), sec0 + s), s=s)) for s in range(3)]
    return pl.pallas_call(
        functools.partial(_gdn_kernel, chunks=chunks),
        grid=(batch, nb),
        in_specs=xs + halos + [
            pl.BlockSpec((tm, LANES), lambda b, n: (b * nb + n, 0)),
            const((GDN_CONV, 3 * BRANCH_WIDTH)),
            const((1, LANES)),
            const((1, LANES)),
            const((BRANCH_WIDTH, BRANCH_WIDTH)),
            const((LANES, BRANCH_WIDTH)),
            const((LANES, BRANCH_WIDTH)),
            const((tm, tm)),
            const((1, LANES)),
        ],
        out_specs=pl.BlockSpec((tm, BRANCH_WIDTH), lambda b, n: (b * nb + n, 0)),
        out_shape=jax.ShapeDtypeStruct((t, BRANCH_WIDTH), BF16),
        scratch_shapes=[pltpu.VMEM((N_HEADS // 2, LANES, LANES), F32)],
        compiler_params=_params(("arbitrary", "arbitrary")),
        name="gdn",
    )(big, big, big, big, big, big, small, conv_w, alog_row, dtb_row, jnp.asarray(grp, BF16),
      jnp.asarray(eg, BF16), jnp.asarray(eb, BF16), jnp.asarray(tri, BF16), gain2)


GLA_LEVELS = 6


def _gla_constants():
    n = CHUNK
    t = np.arange(n)
    blocks = []
    masks = []
    for lev in range(GLA_LEVELS):
        w = 1 << lev
        lower = (t // w) % 2 == 1
        mid_q = (t // w) * w
        cq = (lower[:, None] & (t[None, :] > mid_q[:, None]) & (t[None, :] <= t[:, None])).astype(np.float32)
        mid_k = (t // w + 1) * w
        ck = ((~lower)[:, None] & (t[None, :] > t[:, None]) & (t[None, :] <= mid_k[:, None])).astype(np.float32)
        blocks.append(cq + ck)
        m = lower[:, None] & (~lower)[None, :] & ((t // (2 * w))[:, None] == (t // (2 * w))[None, :])
        masks.append(np.kron(np.eye(2), m.astype(np.float32)))
    masks.append(np.eye(2 * n, dtype=np.float32))
    tri = np.tril(np.ones((n, n), np.float32))
    rev = (t[None, :] > t[:, None]).astype(np.float32)
    blocks += [tri, rev]
    return np.concatenate(blocks, axis=0), np.stack(masks)


def _gla_kernel(x_ref, small_ref, w2_ref, bias_ref, sel_ref, mask_ref, gain_ref, o_ref, state_ref, *, chunks):
    @pl.when(pl.program_id(1) == 0)
    def _():
        state_ref[...] = jnp.zeros_like(state_ref)

    half = _half_masks()
    qk_w = GLA_HEADS * HEAD_DIM
    r = lax.broadcasted_iota(jnp.int32, (2 * LANES, LANES), 0)
    c = lax.broadcasted_iota(jnp.int32, (2 * LANES, LANES), 1)
    state_mask = (r < LANES) == (c < HEAD_DIM)
    zeros_v = jnp.zeros((CHUNK, LANES), F32)

    pairs = range(GLA_HEADS // 2)
    cps = [(ci, p) for ci in range(chunks) for p in pairs]
    crow = lambda ci: slice(ci * CHUNK, (ci + 1) * CHUNK)

    log_a = {}
    for ci in range(chunks):
        y = _dot(small_ref[crow(ci), :].astype(BF16), w2_ref[...]) + bias_ref[...]
        log_a[ci] = _log_sigmoid(y) * (1.0 / GLA_GATE_TAU)

    q2, k2, v_pair, la_split = {}, {}, {}, {}
    for ci, p in cps:
        rows = crow(ci)
        q2[ci, p] = x_ref[rows, p * LANES:(p + 1) * LANES].astype(F32) * (HEAD_DIM ** -0.5)
        k2[ci, p] = x_ref[rows, qk_w + p * LANES:qk_w + (p + 1) * LANES].astype(F32)
        v_pair[ci, p] = x_ref[rows, 2 * qk_w + 2 * p * LANES:2 * qk_w + 2 * (p + 1) * LANES]
        hi, lo = _split2(log_a[ci][:, p * LANES:(p + 1) * LANES])
        la_split[ci, p] = jnp.concatenate([hi, lo], axis=1)

    sums = {}
    for cp in cps:
        s2 = _dot(sel_ref[...], la_split[cp])
        sums[cp] = s2[:, :LANES] + s2[:, LANES:]

    a = {cp: None for cp in cps}
    for lev in range(GLA_LEVELS + 1):
        lhs, rhs = {}, {}
        for cp in cps:
            if lev < GLA_LEVELS:
                dec = jnp.exp(sums[cp][lev * CHUNK:(lev + 1) * CHUNK, :])
                ql, kl = q2[cp] * dec, k2[cp] * dec
            else:
                ql, kl = q2[cp], k2[cp]
            kb = kl.astype(BF16)
            lhs[cp] = _stack_heads(ql, half).astype(BF16)
            rhs[cp] = jnp.concatenate([kb, kb], axis=0)
        prod = {cp: _dot_nt(lhs[cp], rhs[cp]) for cp in cps}
        for cp in cps:
            term = prod[cp] * mask_ref[lev]
            a[cp] = term if a[cp] is None else a[cp] + term

    v_stack, q_dec, k_dec, b_last = {}, {}, {}, {}
    for cp in cps:
        b = sums[cp][GLA_LEVELS * CHUNK:(GLA_LEVELS + 1) * CHUNK, :]
        b_rev = sums[cp][(GLA_LEVELS + 1) * CHUNK:, :]
        vf = v_pair[cp].astype(F32)
        v_stack[cp] = jnp.concatenate(
            [jnp.concatenate([vf[:, :LANES], zeros_v], axis=1),
             jnp.concatenate([zeros_v, vf[:, LANES:]], axis=1)], axis=0).astype(BF16)
        q_dec[cp] = (q2[cp] * jnp.exp(b)).astype(BF16)
        k_dec[cp] = (k2[cp] * jnp.exp(b_rev)).astype(BF16)
        b_last[cp] = jnp.exp(b[CHUNK - 1:CHUNK, :])
    o_loc = {cp: _unstack(_dot(a[cp].astype(BF16), v_stack[cp])) for cp in cps}
    upd = {cp: jnp.where(state_mask, _dot_tn(v_pair[cp], k_dec[cp]), 0.0) for cp in cps}

    state = {p: state_ref[p] for p in pairs}
    for ci in range(chunks):
        o_pair = {p: o_loc[ci, p] + _dot_nt(q_dec[ci, p], state[p].astype(BF16)) for p in pairs}
        for p in pairs:
            state[p] = state[p] * b_last[ci, p] + upd[ci, p]
            for h in range(2):
                oh = o_pair[p][:, h * LANES:(h + 1) * LANES]
                ms = jnp.mean(oh * oh, axis=-1, keepdims=True)
                col0 = (2 * p + h) * LANES
                o_ref[crow(ci), col0:col0 + LANES] = (
                    oh * lax.rsqrt(ms + NORM_EPS) * gain_ref[...]).astype(o_ref.dtype)
    for p in pairs:
        state_ref[p] = state[p]


def _gla(big, small, w2_ext, bias_row, gain_row, batch, seq, chunks=8):
    t = batch * seq
    tm = chunks * CHUNK
    nb = seq // tm
    sel, masks = _gla_constants()
    width = 2 * GLA_HEADS * HEAD_DIM + BRANCH_WIDTH
    const2 = lambda shape: pl.BlockSpec(shape, lambda b, n: (0, 0))
    return pl.pallas_call(
        functools.partial(_gla_kernel, chunks=chunks),
        grid=(batch, nb),
        in_specs=[
            pl.BlockSpec((tm, width), lambda b, n: (b * nb + n, C_GLA * LANES // width)),
            pl.BlockSpec((tm, LANES), lambda b, n: (b * nb + n, 0)),
            const2((LANES, GLA_HEADS * HEAD_DIM)),
            const2((1, GLA_HEADS * HEAD_DIM)),
            const2(sel.shape),
            pl.BlockSpec(masks.shape, lambda b, n: (0, 0, 0)),
            const2((1, LANES)),
        ],
        out_specs=pl.BlockSpec((tm, BRANCH_WIDTH), lambda b, n: (b * nb + n, 0)),
        out_shape=jax.ShapeDtypeStruct((t, BRANCH_WIDTH), BF16),
        scratch_shapes=[pltpu.VMEM((GLA_HEADS // 2, 2 * LANES, LANES), F32)],
        compiler_params=_params(("arbitrary", "arbitrary")),
        name="gla",
    )(big, small, w2_ext, bias_row, jnp.asarray(sel, BF16), jnp.asarray(masks, F32), gain_row)


def _merge_kernel(x_ref, o0_ref, o1_ref, o2_ref, o3_ref, z_ref, ml_ref, gb_ref, wb_ref, wo_ref, out_ref):
    merged = None
    for n, o_ref in enumerate((o0_ref, o1_ref, o2_ref, o3_ref)):
        zs = z_ref[:, n * BRANCH_WIDTH:(n + 1) * BRANCH_WIDTH].astype(F32)
        br = (o_ref[...].astype(F32) * (zs * (1.0 / NEG_LOG2E)) / (1.0 + jnp.exp2(zs))).astype(BF16)
        proj = _dot(br, wb_ref[n])
        gate = 1.0 / (1.0 + jnp.exp2(ml_ref[:, n * D_MODEL:(n + 1) * D_MODEL].astype(F32)
                                     + gb_ref[:, n * D_MODEL:(n + 1) * D_MODEL]))
        term = proj * gate
        merged = term if merged is None else merged + term
    out_ref[...] = x_ref[...] + _dot(merged.astype(BF16), wo_ref[...])


def _merge(x2, outs, big, gate_bias_row, w_branch, w_out, tm=256):
    t = x2.shape[0]
    zw = N_BRANCH * BRANCH_WIDTH
    mw = N_BRANCH * D_MODEL
    o_spec = pl.BlockSpec((tm, BRANCH_WIDTH), lambda i: (i, 0))
    return pl.pallas_call(
        _merge_kernel,
        grid=(t // tm,),
        in_specs=[
            pl.BlockSpec((tm, D_MODEL), lambda i: (i, 0)),
            o_spec, o_spec, o_spec, o_spec,
            pl.BlockSpec((tm, zw), lambda i: (i, C_Z * LANES // zw)),
            pl.BlockSpec((tm, mw), lambda i: (i, C_MERGE * LANES // mw)),
            pl.BlockSpec((1, mw), lambda i: (0, 0)),
            pl.BlockSpec((N_BRANCH, BRANCH_WIDTH, D_MODEL), lambda i: (0, 0, 0), pipeline_mode=pl.Buffered(1)),
            pl.BlockSpec((D_MODEL, D_MODEL), lambda i: (0, 0), pipeline_mode=pl.Buffered(1)),
        ],
        out_specs=pl.BlockSpec((tm, D_MODEL), lambda i: (i, 0)),
        out_shape=jax.ShapeDtypeStruct((t, D_MODEL), F32),
        compiler_params=_params(("arbitrary",)),
        name="merge_out",
    )(x2, *outs, big, big, gate_bias_row, w_branch, w_out)


def _reorder_w_in(w_in):
    sizes = (3 * BRANCH_WIDTH, N_HEADS, BRANCH_WIDTH, 3 * BRANCH_WIDTH, BRANCH_WIDTH,
             3 * BRANCH_WIDTH, N_HEADS, N_HEADS, BRANCH_WIDTH,
             2 * GLA_HEADS * HEAD_DIM, BRANCH_WIDTH, GLA_GATE_RANK, BRANCH_WIDTH, N_BRANCH * D_MODEL)
    (fox_qkv, fox_f, fox_z, sb_qkv, sb_z, gdn_qkv, gdn_a, gdn_b, gdn_z,
     gla_qk, gla_v, gla_g, gla_z, merge) = jnp.split(w_in, np.cumsum(sizes)[:-1].tolist(), axis=1)
    gate_cols = jnp.concatenate([merge, fox_z, sb_z, gdn_z, gla_z], axis=1) * NEG_LOG2E
    big = jnp.concatenate([gate_cols, gla_qk, gla_v, gdn_qkv, fox_qkv, sb_qkv], axis=1)
    pad = jnp.zeros((w_in.shape[0], LANES - (3 * N_HEADS + GLA_GATE_RANK)), w_in.dtype)
    small = jnp.concatenate([fox_f, gdn_a, gdn_b, gla_g, pad], axis=1)
    return big.astype(BF16), small.astype(BF16)


def _lane_row(values, offset):
    row = jnp.zeros((1, LANES), F32)
    return row.at[0, offset:offset + values.shape[0]].set(values.astype(F32))


def _layer(x2, batch, seq, norm_g, w_in, fox_f_bias, fox_q_norm, fox_k_norm, sb_q_norm, sb_k_norm,
           gdn_conv_w, gdn_a_log, gdn_dt_bias, gdn_o_norm, gla_gate_w2, gla_gate_bias, gla_o_norm,
           gate_bias, w_branch, w_out):
    w_big, w_small = _reorder_w_in(w_in)
    g_row = norm_g.reshape(1, D_MODEL).astype(F32)
    t = batch * seq
    big, small = _norm_proj(x2, g_row, w_big, w_small, min(2048, t), 512)

    pair = lambda g: jnp.tile(g.astype(F32), 2).reshape(1, LANES)

    c = _fox_cum(small, _lane_row(fox_f_bias, L_FOXF), batch, seq)
    tq = tk = min(512, seq)
    c3 = c.reshape(batch, seq, LANES)[:, :, :N_HEADS]
    first = jnp.transpose(c3[:, 0::tq, :], (0, 2, 1))
    last = jnp.transpose(c3[:, tk - 1::tk, :], (0, 2, 1))
    logit_bound = lambda gq, gk: 1.02 * HEAD_DIM ** 0.5 * jnp.max(jnp.abs(gq)) * jnp.max(jnp.abs(gk))
    qk_bound = logit_bound(fox_q_norm, fox_k_norm)
    thr = EXP_UNDERFLOW + 2.0 * qk_bound + 1.0
    tab = jnp.concatenate([first.reshape(-1), last.reshape(-1), thr.reshape(1), qk_bound.reshape(1)]).astype(F32)
    o_fox = _fox_attention(big, c, tab, pair(fox_q_norm), pair(fox_k_norm), batch, seq, tq, tk)

    o_sb = _sb_attention(big, logit_bound(sb_q_norm, sb_k_norm).reshape(1).astype(F32),
                         pair(sb_q_norm), pair(sb_k_norm), batch, seq, min(256, seq))

    o_gdn = _gdn(big, small, gdn_conv_w.astype(F32), _lane_row(gdn_a_log, L_GDNA),
                 _lane_row(gdn_dt_bias, L_GDNA), pair(gdn_o_norm), batch, seq)

    w2_ext = jnp.zeros((LANES, GLA_HEADS * HEAD_DIM), F32).at[L_GLAG:L_GLAG + GLA_GATE_RANK].set(
        gla_gate_w2.astype(F32)).astype(BF16)
    o_gla = _gla(big, small, w2_ext, gla_gate_bias.reshape(1, -1).astype(F32),
                 gla_o_norm.reshape(1, LANES).astype(F32), batch, seq)

    return _merge(x2, (o_fox, o_sb, o_gdn, o_gla), big, gate_bias.reshape(1, -1).astype(F32) * NEG_LOG2E,
                  w_branch.astype(BF16), w_out.astype(BF16), min(512, t))


def kernel(x, norm_g, w_in, fox_f_bias, fox_q_norm, fox_k_norm, sb_q_norm, sb_k_norm, gdn_conv_w,
           gdn_a_log, gdn_dt_bias, gdn_o_norm, gla_gate_w2, gla_gate_bias, gla_o_norm, gate_bias,
           w_branch, w_out):
    batch, seq, d = x.shape
    x2 = x.reshape(batch * seq, d)
    params = (norm_g, w_in, fox_f_bias, fox_q_norm, fox_k_norm, sb_q_norm, sb_k_norm, gdn_conv_w,
              gdn_a_log, gdn_dt_bias, gdn_o_norm, gla_gate_w2, gla_gate_bias, gla_o_norm, gate_bias,
              w_branch, w_out)
    for l in range(norm_g.shape[0]):
        x2 = _layer(x2, batch, seq, *[p[l] for p in params])
    return x2.reshape(batch, seq, d)
```

```python
import functools

import numpy as np
import jax
import jax.numpy as jnp
from jax import lax
from jax.experimental import pallas as pl
from jax.experimental.pallas import tpu as pltpu

F32 = jnp.float32
BF16 = jnp.bfloat16

LANES = 128
HEAD_DIM = 64
D_MODEL = 1024
BRANCH_WIDTH = 512
N_BRANCH = 4
N_HEADS = 8
GLA_HEADS = 4
GLA_GATE_RANK = 16
GLA_GATE_TAU = 16.0
GDN_CONV = 4
CHUNK = 64
NORM_EPS = 1e-6
VMEM_LIMIT = 48 * 1024 * 1024
EXP_UNDERFLOW = 104.0
NEG_BIG = -1e30
NEG_LOG2E = -1.4426950408889634
STATIC_LOGIT_LIMIT = 20.0

C_MERGE, C_Z, C_GLA, C_GDN, C_FOX, C_SB, C_END = 0, 32, 48, 56, 68, 80, 92
N_BIG = C_END * LANES
L_FOXF, L_GDNA, L_GDNB, L_GLAG = 0, 8, 16, 24


def _dot(a, b):
    return jnp.dot(a, b, preferred_element_type=F32)


def _dot_nt(a, b):
    return lax.dot_general(a, b, (((1,), (1,)), ((), ())), preferred_element_type=F32)


def _dot_tn(a, b):
    return lax.dot_general(a, b, (((0,), (0,)), ((), ())), preferred_element_type=F32)


def _split2(x):
    hi = x.astype(BF16)
    lo = (x - hi.astype(F32)).astype(BF16)
    return hi, lo


def _split3(x):
    hi = x.astype(BF16)
    r = x - hi.astype(F32)
    lo = r.astype(BF16)
    lo2 = (r - lo.astype(F32)).astype(BF16)
    return hi, lo, lo2


def _sel_dot(sel, x):
    hi, lo, lo2 = _split3(x)
    return _dot(sel, hi) + _dot(sel, lo) + _dot(sel, lo2)


def _dot_sel(x, sel):
    hi, lo, lo2 = _split3(x)
    return _dot(hi, sel) + _dot(lo, sel) + _dot(lo2, sel)


def _softplus(x):
    return jnp.maximum(x, 0.0) + jnp.log(1.0 + jnp.exp(-jnp.abs(x)))


def _log_sigmoid(x):
    return jnp.minimum(x, 0.0) - jnp.log(1.0 + jnp.exp(-jnp.abs(x)))


def _sigmoid(x):
    return 1.0 / (1.0 + jnp.exp(-x))


def _silu(x):
    return x / (1.0 + jnp.exp(-x))


def _lane_iota(shape):
    return lax.broadcasted_iota(jnp.int32, shape, len(shape) - 1)


def _half_masks():
    lane = _lane_iota((1, LANES))
    return lane < HEAD_DIM, lane >= HEAD_DIM


def _params(semantics):
    return pltpu.CompilerParams(dimension_semantics=semantics, vmem_limit_bytes=VMEM_LIMIT)


def _proj_kernel(x_ref, g_ref, w_ref, ws_ref, o_ref, os_ref, h_ref):
    @pl.when(pl.program_id(1) == 0)
    def _():
        x = x_ref[...]
        ms = jnp.mean(x * x, axis=-1, keepdims=True)
        h_ref[...] = (x * lax.rsqrt(ms + NORM_EPS) * g_ref[...]).astype(BF16)
        os_ref[...] = _dot(h_ref[...], ws_ref[...])

    o_ref[...] = _dot(h_ref[...], w_ref[...]).astype(o_ref.dtype)


def _norm_proj(x2, g, w, w_small, tm, tn):
    t, d = x2.shape
    n = w.shape[1]
    return pl.pallas_call(
        _proj_kernel,
        grid=(t // tm, n // tn),
        in_specs=[
            pl.BlockSpec((tm, d), lambda i, j: (i, 0)),
            pl.BlockSpec((1, d), lambda i, j: (0, 0)),
            pl.BlockSpec((d, tn), lambda i, j: (0, j)),
            pl.BlockSpec((d, LANES), lambda i, j: (0, 0)),
        ],
        out_specs=[pl.BlockSpec((tm, tn), lambda i, j: (i, j)),
                   pl.BlockSpec((tm, LANES), lambda i, j: (i, 0))],
        out_shape=[jax.ShapeDtypeStruct((t, n), BF16), jax.ShapeDtypeStruct((t, LANES), F32)],
        scratch_shapes=[pltpu.VMEM((tm, d), BF16)],
        compiler_params=_params(("arbitrary", "arbitrary")),
        name="norm_proj",
    )(x2, g, w, w_small)


def _fox_cum_kernel(f_ref, bias_ref, tri_ref, c_ref, carry_ref):
    @pl.when(pl.program_id(1) == 0)
    def _():
        carry_ref[...] = jnp.zeros_like(carry_ref)

    lf = _log_sigmoid(f_ref[...] + bias_ref[...])
    cs = _sel_dot(tri_ref[...], lf) + carry_ref[...]
    c_ref[...] = cs
    carry_ref[...] = cs[-1:, :]


def _fox_cum(small, bias_row, batch, seq, tr=512):
    tri = jnp.asarray(np.tril(np.ones((tr, tr), np.float32)), BF16)
    nb = seq // tr
    return pl.pallas_call(
        _fox_cum_kernel,
        grid=(batch, nb),
        in_specs=[
            pl.BlockSpec((tr, LANES), lambda b, r: (b * nb + r, 0)),
            pl.BlockSpec((1, LANES), lambda b, r: (0, 0)),
            pl.BlockSpec((tr, tr), lambda b, r: (0, 0)),
        ],
        out_specs=pl.BlockSpec((tr, LANES), lambda b, r: (b * nb + r, 0)),
        out_shape=jax.ShapeDtypeStruct(small.shape, F32),
        scratch_shapes=[pltpu.VMEM((1, LANES), F32)],
        compiler_params=_params(("arbitrary", "arbitrary")),
        name="fox_cum",
    )(small, bias_row, tri)


def _pair_rms(x, gain, half0, half1):
    sq = x * x
    ms0 = jnp.sum(jnp.where(half0, sq, 0.0), axis=-1, keepdims=True) * (1.0 / HEAD_DIM)
    ms1 = jnp.sum(jnp.where(half1, sq, 0.0), axis=-1, keepdims=True) * (1.0 / HEAD_DIM)
    r = jnp.where(half0, lax.rsqrt(ms0 + NORM_EPS), lax.rsqrt(ms1 + NORM_EPS))
    return x * r * gain


def _lane_pick(x, idx):
    lane = _lane_iota((1, LANES))
    return jnp.sum(jnp.where(lane == idx, x, 0.0), axis=-1, keepdims=True)


def _aug_lanes(c_col, base, key_side):
    lane = _lane_iota((1, LANES))
    hi, lo, lo2 = [v.astype(F32) for v in _split3(c_col)]
    if key_side:
        hi, lo, lo2 = -hi, -lo, -lo2
        vals, ones = (base + 3, base + 4, base + 5), (base, base + 1, base + 2)
    else:
        vals, ones = (base, base + 1, base + 2), (base + 3, base + 4, base + 5)
    out = jnp.where((lane == ones[0]) | (lane == ones[1]) | (lane == ones[2]), 1.0, 0.0)
    out = jnp.where(lane == vals[0], hi, out)
    out = jnp.where(lane == vals[1], lo, out)
    out = jnp.where(lane == vals[2], lo2, out)
    return out


def _fox_kernel(tab_ref, q_ref, k_ref, v_ref, cq_ref, ck_ref, gq_ref, gk_ref, o_ref,
                kaug_ref, vaug_ref, acc_ref, m_ref, l_ref, *, seq, tq, tk, rows_per_pass):
    b, p, i = pl.program_id(0), pl.program_id(1), pl.program_id(2)
    nq, nk = seq // tq, seq // tk
    nb = pl.num_programs(0)
    half = _half_masks()
    lane = _lane_iota((1, LANES))
    scale = HEAD_DIM ** -0.5
    spare = [HEAD_DIM * (1 - h) for h in range(2)]

    @pl.when(i == 0)
    def _():
        def fill(r, carry):
            rows = pl.ds(pl.multiple_of(r * rows_per_pass, rows_per_pass), rows_per_pass)
            kn = _pair_rms(k_ref[rows, :].astype(F32), gk_ref[...], *half)
            cc = ck_ref[rows, :]
            vv = v_ref[rows, :]
            for h in range(2):
                aug = _aug_lanes(_lane_pick(cc, 2 * p + h), spare[h], True)
                kaug_ref[h, rows, :] = jnp.where(half[h], kn, aug).astype(BF16)
                ones = jnp.where(lane == spare[h], 1.0, 0.0).astype(BF16)
                vaug_ref[h, rows, :] = jnp.where(half[h], vv, ones)
            return carry
        lax.fori_loop(0, seq // rows_per_pass, fill, 0)

    first_off = N_HEADS * nb * nq
    thr = tab_ref[first_off + N_HEADS * nb * nk]
    bound = tab_ref[first_off + N_HEADS * nb * nk + 1]
    fast = bound < STATIC_LOGIT_LIMIT

    qn = _pair_rms(q_ref[...].astype(F32), gq_ref[...], *half) * scale
    cq = cq_ref[...]
    tile_i = i // (tk // tq)
    row = lax.broadcasted_iota(jnp.int32, (tq, tk), 0) + (i % (tk // tq)) * tq
    col = lax.broadcasted_iota(jnp.int32, (tq, tk), 1)
    causal = col <= row

    c_first = [tab_ref[(b * N_HEADS + 2 * p + h) * nq + i] for h in range(2)]

    def cond(j):
        jj = jnp.maximum(j, 0)
        need = [c_first[h] - tab_ref[first_off + (b * N_HEADS + 2 * p + h) * nk + jj] >= -thr for h in range(2)]
        return jnp.logical_and(j >= 0, jnp.logical_or(need[0], need[1]))

    def walk(tile):
        tile(tile_i, True)

        def body(j):
            tile(j, False)
            return j - 1
        lax.while_loop(cond, body, tile_i - 1)

    def q_aug(shift):
        return [jnp.where(half[h], qn, _aug_lanes(_lane_pick(cq, 2 * p + h) - shift, spare[h], False)).astype(BF16)
                for h in range(2)]

    @pl.when(fast)
    def _():
        qa = q_aug(bound)
        acc_ref[...] = jnp.zeros_like(acc_ref)

        def tile(j, masked):
            rows = pl.ds(pl.multiple_of(j * tk, tk), tk)
            s = [_dot_nt(qa[h], kaug_ref[h, rows, :]) for h in range(2)]
            if masked:
                s = [jnp.where(causal, sh, NEG_BIG) for sh in s]
            pr = [jnp.exp(sh).astype(BF16) for sh in s]
            for h in range(2):
                acc_ref[h] = acc_ref[h] + _dot(pr[h], vaug_ref[h, rows, :])

        walk(tile)
        out = [acc_ref[h] / acc_ref[h][:, spare[h]:spare[h] + 1] for h in range(2)]
        o_ref[...] = jnp.where(half[0], out[0], out[1]).astype(o_ref.dtype)

    @pl.when(jnp.logical_not(fast))
    def _():
        qa = q_aug(0.0)
        acc_ref[...] = jnp.zeros_like(acc_ref)
        m_ref[...] = jnp.full_like(m_ref, NEG_BIG)
        l_ref[...] = jnp.zeros_like(l_ref)

        def tile(j, masked):
            rows = pl.ds(pl.multiple_of(j * tk, tk), tk)
            for h in range(2):
                s = _dot_nt(qa[h], kaug_ref[h, rows, :])
                if masked:
                    s = jnp.where(causal, s, NEG_BIG)
                m_prev = m_ref[h]
                m_new = jnp.maximum(m_prev, jnp.max(s, axis=1, keepdims=True))
                alpha = jnp.exp(m_prev - m_new)
                pr = jnp.exp(s - m_new)
                l_ref[h] = alpha * l_ref[h] + jnp.sum(pr, axis=1, keepdims=True)
                acc_ref[h] = alpha * acc_ref[h] + _dot(pr.astype(BF16), vaug_ref[h, rows, :])
                m_ref[h] = m_new

        walk(tile)
        out = [acc_ref[h] / l_ref[h] for h in range(2)]
        o_ref[...] = jnp.where(half[0], out[0], out[1]).astype(o_ref.dtype)


def _fox_attention(big, c, tab, gq2, gk2, batch, seq, tq, tk):
    nq = seq // tq
    t = batch * seq
    kern = functools.partial(_fox_kernel, seq=seq, tq=tq, tk=tk, rows_per_pass=min(512, seq))
    grid_spec = pltpu.PrefetchScalarGridSpec(
        num_scalar_prefetch=1,
        grid=(batch, N_HEADS // 2, nq),
        in_specs=[
            pl.BlockSpec((tq, LANES), lambda b, p, i, tab: (b * nq + i, C_FOX + p)),
            pl.BlockSpec((seq, LANES), lambda b, p, i, tab: (b, C_FOX + 4 + p)),
            pl.BlockSpec((seq, LANES), lambda b, p, i, tab: (b, C_FOX + 8 + p)),
            pl.BlockSpec((tq, LANES), lambda b, p, i, tab: (b * nq + i, 0)),
            pl.BlockSpec((seq, LANES), lambda b, p, i, tab: (b, 0)),
            pl.BlockSpec((1, LANES), lambda b, p, i, tab: (0, 0)),
            pl.BlockSpec((1, LANES), lambda b, p, i, tab: (0, 0)),
        ],
        out_specs=pl.BlockSpec((tq, LANES), lambda b, p, i, tab: (b * nq + i, p)),
        scratch_shapes=[
            pltpu.VMEM((2, seq, LANES), BF16),
            pltpu.VMEM((2, seq, LANES), BF16),
            pltpu.VMEM((2, tq, LANES), F32),
            pltpu.VMEM((2, tq, 1), F32),
            pltpu.VMEM((2, tq, 1), F32),
        ],
    )
    return pl.pallas_call(
        kern,
        grid_spec=grid_spec,
        out_shape=jax.ShapeDtypeStruct((t, BRANCH_WIDTH), BF16),
        compiler_params=_params(("arbitrary", "arbitrary", "arbitrary")),
        name="fox_attention",
    )(tab, big, big, big, c, c, gq2, gk2)


SB_MASKED_LOGIT = -1e4


def _sb_kernel(bound_ref, q_ref, k_ref, v_ref, gq_ref, gk_ref, lower_ref, o_ref,
               kn_ref, acc_ref, run_ref, flag_ref, *, seq, tq, nsub, rows_per_pass):
    i = pl.program_id(2)
    half = _half_masks()
    scale = HEAD_DIM ** -0.5

    @pl.when(i == 0)
    def _():
        def fill(r, carry):
            rows = pl.ds(pl.multiple_of(r * rows_per_pass, rows_per_pass), rows_per_pass)
            kn_ref[rows, :] = _pair_rms(k_ref[rows, :].astype(F32), gk_ref[...], *half).astype(BF16)
            return carry
        lax.fori_loop(0, seq // rows_per_pass, fill, 0)

    qn = _pair_rms(q_ref[...].astype(F32), gq_ref[...], *half) * scale
    qh = [[jnp.where(half[h], qn[s * tq:(s + 1) * tq], 0.0).astype(BF16) for h in range(2)] for s in range(nsub)]
    acc_ref[...] = jnp.zeros_like(acc_ref)
    run_ref[...] = jnp.zeros_like(run_ref)
    block = [i * nsub + s for s in range(nsub)]
    first_block = [jnp.maximum(blk - 1, 0) for blk in block]

    def step(tiles, bounded):
        kk, vv, low, z = {}, {}, {}, {}
        for s, start, width, offset in tiles:
            rows = pl.ds(pl.multiple_of(start, tq), width)
            kk[s], vv[s], low[s] = kn_ref[rows, :], v_ref[rows, :], lower_ref[:width, :width]
        for s, start, width, offset in tiles:
            for h in range(2):
                z[s, h] = _dot_nt(qh[s][h], kk[s])
        for s, start, width, offset in tiles:
            if offset is not None:
                row = lax.broadcasted_iota(jnp.int32, (tq, width), 0) + offset
                col = lax.broadcasted_iota(jnp.int32, (tq, width), 1)
                for h in range(2):
                    z[s, h] = jnp.where(col < row, z[s, h], SB_MASKED_LOGIT)
        keys = list(z)
        if bounded:
            log_rest = {k: -jnp.log(1.0 + jnp.exp(z[k])) for k in keys}
        else:
            log_rest = {k: -_softplus(z[k]) for k in keys}
        inner = {k: _dot(log_rest[k].astype(BF16), low[k[0]]) for k in keys}
        w = {k: jnp.exp((z[k] + run_ref[k[0], k[1]]) + inner[k]).astype(BF16) for k in keys}
        for k in keys:
            acc_ref[k[0], k[1]] = acc_ref[k[0], k[1]] + _dot(w[k], vv[k[0]])
        run = {k: run_ref[k[0], k[1]] + jnp.sum(log_rest[k], axis=1, keepdims=True) for k in keys}
        for k in keys:
            run_ref[k[0], k[1]] = run[k]
        for s, start, width, offset in tiles:
            flag_ref[s] = jnp.maximum(jnp.max(run[s, 0]), jnp.max(run[s, 1]))

    def walk(bounded):
        step([(s, first_block[s] * tq, 2 * tq, (block[s] - first_block[s]) * tq) for s in range(nsub)], bounded)
        for s in range(nsub):
            def cond(j, s=s):
                return jnp.logical_and(j >= 0, flag_ref[s] > -EXP_UNDERFLOW)

            def body(j, s=s):
                step([(s, j * tq, tq, None)], bounded)
                return j - 1

            lax.while_loop(cond, body, first_block[s] - 1)
            o_ref[s * tq:(s + 1) * tq, :] = jnp.where(half[0], acc_ref[s, 0], acc_ref[s, 1]).astype(o_ref.dtype)

    bounded = bound_ref[0] < STATIC_LOGIT_LIMIT
    pl.when(bounded)(lambda: walk(True))
    pl.when(jnp.logical_not(bounded))(lambda: walk(False))


def _sb_attention(big, bound, gq2, gk2, batch, seq, tq=256):
    nsub = 2 if seq % (2 * tq) == 0 else 1
    tb = nsub * tq
    nq = seq // tb
    t = batch * seq
    lower = jnp.asarray(np.tril(np.ones((2 * tq, 2 * tq), np.float32)), BF16)
    kern = functools.partial(_sb_kernel, seq=seq, tq=tq, nsub=nsub, rows_per_pass=min(512, seq))
    return pl.pallas_call(
        kern,
        grid=(batch, N_HEADS // 2, nq),
        in_specs=[
            pl.BlockSpec(memory_space=pltpu.SMEM),
            pl.BlockSpec((tb, LANES), lambda b, p, i: (b * nq + i, C_SB + p)),
            pl.BlockSpec((seq, LANES), lambda b, p, i: (b, C_SB + 4 + p)),
            pl.BlockSpec((seq, LANES), lambda b, p, i: (b, C_SB + 8 + p)),
            pl.BlockSpec((1, LANES), lambda b, p, i: (0, 0)),
            pl.BlockSpec((1, LANES), lambda b, p, i: (0, 0)),
            pl.BlockSpec((2 * tq, 2 * tq), lambda b, p, i: (0, 0)),
        ],
        out_specs=pl.BlockSpec((tb, LANES), lambda b, p, i: (b * nq + i, p)),
        out_shape=jax.ShapeDtypeStruct((t, BRANCH_WIDTH), BF16),
        scratch_shapes=[
            pltpu.VMEM((seq, LANES), BF16),
            pltpu.VMEM((nsub, 2, tq, LANES), F32),
            pltpu.VMEM((nsub, 2, tq, 1), F32),
            pltpu.SMEM((nsub,), F32),
        ],
        compiler_params=_params(("arbitrary", "arbitrary", "arbitrary")),
        name="sb_attention",
    )(bound, big, big, big, gq2, gk2, lower)


GDN_HALO = 16


def _gdn_prep_vpu(first, xq_ref, xk_ref, xv_ref, hq_ref, hk_ref, hv_ref, small_ref, cw_ref, alog_ref, dtb_ref,
                  grp_ref, eg_ref, eb_ref, tri_ref):
    def conv_silu(x_ref, halo_ref, sec):
        halo = jnp.where(first, 0.0, halo_ref[...].astype(F32))
        xx = jnp.concatenate([halo, x_ref[...].astype(F32)], axis=0)
        cols = slice(sec * BRANCH_WIDTH, (sec + 1) * BRANCH_WIDTH)
        y = cw_ref[GDN_CONV - 1:GDN_CONV, cols] * xx[GDN_HALO:, :]
        for d in range(1, GDN_CONV):
            y = y + cw_ref[GDN_CONV - 1 - d:GDN_CONV - d, cols] * pltpu.roll(xx, d, 0)[GDN_HALO:, :]
        return _silu(y)

    q, k, v = conv_silu(xq_ref, hq_ref, 0), conv_silu(xk_ref, hk_ref, 1), conv_silu(xv_ref, hv_ref, 2)
    sm = small_ref[...]
    g = -jnp.exp(alog_ref[...]) * _softplus(sm + dtb_ref[...])
    return dict(q=q, k=k, v=v, q_sq=_split2(q * q), k_sq=_split2(k * k), g=g, beta=_split2(_sigmoid(sm)))


def _gdn_prep_mxu(part, grp_ref, eg_ref, eb_ref, tri_ref):
    def l2(u, sq):
        ss = _dot(sq[0], grp_ref[...]) + _dot(sq[1], grp_ref[...])
        return u * lax.rsqrt(ss + NORM_EPS)

    q = l2(part["q"], part["q_sq"]) * (HEAD_DIM ** -0.5)
    k = l2(part["k"], part["k_sq"])
    gc = _dot_sel(_sel_dot(tri_ref[...], part["g"]), eg_ref[...])
    beta = sum(_dot(term, eb_ref[...]) for term in part["beta"])
    return q, k, part["v"], gc, beta


def _stack_heads(x, half):
    return jnp.concatenate([jnp.where(half[0], x, 0.0), jnp.where(half[1], x, 0.0)], axis=0)


def _unstack(x):
    return x[:CHUNK, :] + x[CHUNK:, :]


GDN_PREP_INPUTS = 7
GDN_PREP_CONSTS = 7


def _gdn_kernel(*refs, chunks):
    tile = refs[:GDN_PREP_INPUTS]
    consts = refs[GDN_PREP_INPUTS:GDN_PREP_INPUTS + GDN_PREP_CONSTS]
    gain_ref, o_ref, state_ref = refs[GDN_PREP_INPUTS + GDN_PREP_CONSTS:]
    pairs = range(N_HEADS // 2)
    first = pl.program_id(1) == 0

    @pl.when(first)
    def _():
        state_ref[...] = jnp.zeros_like(state_ref)

    whole = _gdn_prep_mxu(_gdn_prep_vpu(first, *tile, *consts), *consts[3:])
    vals = [[a[c * CHUNK:(c + 1) * CHUNK, :] for a in whole] for c in range(chunks)]
    state = _gdn_scan_tile(vals, {p: state_ref[p] for p in pairs}, gain_ref, o_ref, chunks)
    for p in pairs:
        state_ref[p] = state[p]


def _gdn_scan_tile(vals, state, gain_ref, o_ref, chunks):
    half = _half_masks()
    n2 = 2 * CHUNK
    r = lax.broadcasted_iota(jnp.int32, (n2, n2), 0)
    c = lax.broadcasted_iota(jnp.int32, (n2, n2), 1)
    same = (r < CHUNK) == (c < CHUNK)
    lower = same & (c <= r)
    strict = same & (c < r)
    eye = (r == c).astype(F32)
    lane = _lane_iota((1, LANES))
    sub = lane % HEAD_DIM

    pairs = range(N_HEADS // 2)
    cps = [(ci, p) for ci in range(chunks) for p in pairs]
    win = {cp: (slice(cp[0] * CHUNK, (cp[0] + 1) * CHUNK), slice(cp[1] * LANES, (cp[1] + 1) * LANES))
           for cp in cps}

    diff_l, diff_r, kb_s, q_s, kk2, rhs, k_dec, q_dec, last = {}, {}, {}, {}, {}, {}, {}, {}, {}
    for cp in cps:
        rows, cols = win[cp]
        q2, k2, v2, gc2, beta2 = [a[:, cols] for a in vals[cp[0]]]
        hi, lo, lo2 = [u.astype(F32) for u in _split3(gc2)]
        one = jnp.where((sub >= 3) & (sub < 6), 1.0, 0.0)
        la = jnp.where(sub == 0, hi, jnp.where(sub == 1, lo, jnp.where(sub == 2, lo2, one)))
        one = jnp.where(sub < 3, 1.0, 0.0)
        ra = jnp.where(sub == 3, -hi, jnp.where(sub == 4, -lo, jnp.where(sub == 5, -lo2, one))).astype(BF16)
        diff_l[cp] = _stack_heads(la, half).astype(BF16)
        diff_r[cp] = jnp.concatenate([ra, ra], axis=0)
        eg = jnp.exp(gc2)
        kb = k2 * beta2
        kk = k2.astype(BF16)
        kk2[cp] = jnp.concatenate([kk, kk], axis=0)
        kb_s[cp] = _stack_heads(kb, half).astype(BF16)
        q_s[cp] = _stack_heads(q2, half).astype(BF16)
        rhs[cp] = jnp.concatenate([_stack_heads(v2 * beta2, half), _stack_heads(kb * eg, half)],
                                  axis=1).astype(BF16)
        k_dec[cp] = (k2 * jnp.exp(gc2[CHUNK - 1:CHUNK, :] - gc2)).astype(BF16)
        q_dec[cp] = q2 * eg
        last[cp] = eg[CHUNK - 1:CHUNK, :]

    diff = {cp: _dot_nt(diff_l[cp], diff_r[cp]) for cp in cps}
    akk = {cp: _dot_nt(kb_s[cp], kk2[cp]) for cp in cps}
    aqk = {cp: _dot_nt(q_s[cp], kk2[cp]) for cp in cps}

    x, pw, intra = {}, {}, {}
    for cp in cps:
        decay = jnp.where(lower, jnp.exp(jnp.minimum(diff[cp], 0.0)), 0.0)
        a = jnp.where(strict, akk[cp] * decay, 0.0)
        intra[cp] = (aqk[cp] * decay).astype(BF16)
        x[cp] = eye - a
        pw[cp] = a

    for _ in range(5):
        for cp in cps:
            pb = pw[cp].astype(BF16)
            pw[cp] = _dot(pb, pb)
        for cp in cps:
            x[cp] = x[cp] + _dot(x[cp].astype(BF16), pw[cp].astype(BF16))

    uw = {cp: _dot(x[cp].astype(BF16), rhs[cp]) for cp in cps}
    u2 = {cp: _unstack(uw[cp][:, :LANES]) for cp in cps}
    w2 = {cp: _unstack(uw[cp][:, LANES:]) for cp in cps}

    uw_stack = {cp: jnp.concatenate([_stack_heads(u2[cp], half), _stack_heads(w2[cp], half)],
                                    axis=1).astype(BF16) for cp in cps}
    uw_flat = {cp: jnp.concatenate([u2[cp], w2[cp]], axis=1).astype(BF16) for cp in cps}
    iuw = {cp: _dot(intra[cp], uw_stack[cp]) for cp in cps}
    kuw = {cp: _dot_tn(k_dec[cp], uw_flat[cp]) for cp in cps}
    o_loc, lhs, s_add = {}, {}, {}
    for cp in cps:
        o_loc[cp] = _unstack(iuw[cp][:, :LANES])
        q_eff = q_dec[cp] - _unstack(iuw[cp][:, LANES:])
        m_eff = jnp.where(same, -kuw[cp][:, LANES:], 0.0)
        lhs[cp] = jnp.concatenate([q_eff, m_eff], axis=0).astype(BF16)
        s_add[cp] = jnp.where(same, kuw[cp][:, :LANES], 0.0)

    state = dict(state)
    for ci in range(chunks):
        res = {}
        for p in pairs:
            hi, lo = _split2(state[p])
            res[p] = _dot(lhs[(ci, p)], jnp.concatenate([hi, lo], axis=1))
        for p in pairs:
            cp = (ci, p)
            rr = res[p][:, :LANES] + res[p][:, LANES:]
            o2 = rr[:CHUNK, :] + o_loc[cp]
            state[p] = state[p] * last[cp] + rr[CHUNK:, :] + s_add[cp]
            rows, cols = win[cp]
            o_ref[rows, cols] = _pair_rms(o2, gain_ref[...], *half).astype(o_ref.dtype)
    return state


def _gdn(big, small, conv_w, alog_row, dtb_row, gain2, batch, seq, chunks=8):
    t = batch * seq
    tm = chunks * CHUNK
    nb = seq // tm
    ratio = tm // GDN_HALO
    grp = np.kron(np.eye(N_HEADS, dtype=np.float32), np.ones((HEAD_DIM, HEAD_DIM), np.float32))
    eg = np.zeros((LANES, BRANCH_WIDTH), np.float32)
    eb = np.zeros((LANES, BRANCH_WIDTH), np.float32)
    for h in range(N_HEADS):
        eg[L_GDNA + h, h * HEAD_DIM:(h + 1) * HEAD_DIM] = 1.0
        eb[L_GDNB + h, h * HEAD_DIM:(h + 1) * HEAD_DIM] = 1.0
    tri = np.kron(np.eye(chunks, dtype=np.float32), np.tril(np.ones((CHUNK, CHUNK), np.float32)))
    const = lambda shape: pl.BlockSpec(shape, lambda b, n: (0, 0))
    sec0 = C_GDN * LANES // BRANCH_WIDTH
    xs = [pl.BlockSpec((tm, BRANCH_WIDTH), functools.partial(lambda b, n, s: (b * nb + n, sec0 + s), s=s))
          for s in range(3)]
    halos = [pl.BlockSpec((GDN_HALO, BRANCH_WIDTH), functools.partial(
        lambda b, n, s: (jnp.maximum((b * nb + n) * ratio - 1, 0), sec0 + s), s=s)) for s in range(3)]
    return pl.pallas_call(
        functools.partial(_gdn_kernel, chunks=chunks),
        grid=(batch, nb),
        in_specs=xs + halos + [
            pl.BlockSpec((tm, LANES), lambda b, n: (b * nb + n, 0)),
            const((GDN_CONV, 3 * BRANCH_WIDTH)),
            const((1, LANES)),
            const((1, LANES)),
            const((BRANCH_WIDTH, BRANCH_WIDTH)),
            const((LANES, BRANCH_WIDTH)),
            const((LANES, BRANCH_WIDTH)),
            const((tm, tm)),
            const((1, LANES)),
        ],
        out_specs=pl.BlockSpec((tm, BRANCH_WIDTH), lambda b, n: (b * nb + n, 0)),
        out_shape=jax.ShapeDtypeStruct((t, BRANCH_WIDTH), BF16),
        scratch_shapes=[pltpu.VMEM((N_HEADS // 2, LANES, LANES), F32)],
        compiler_params=_params(("arbitrary", "arbitrary")),
        name="gdn",
    )(big, big, big, big, big, big, small, conv_w, alog_row, dtb_row, jnp.asarray(grp, BF16),
      jnp.asarray(eg, BF16), jnp.asarray(eb, BF16), jnp.asarray(tri, BF16), gain2)


GLA_LEVELS = 6


def _gla_constants():
    n = CHUNK
    t = np.arange(n)
    blocks = []
    masks = []
    for lev in range(GLA_LEVELS):
        w = 1 << lev
        lower = (t // w) % 2 == 1
        mid_q = (t // w) * w
        cq = (lower[:, None] & (t[None, :] > mid_q[:, None]) & (t[None, :] <= t[:, None])).astype(np.float32)
        mid_k = (t // w + 1) * w
        ck = ((~lower)[:, None] & (t[None, :] > t[:, None]) & (t[None, :] <= mid_k[:, None])).astype(np.float32)
        blocks.append(cq + ck)
        m = lower[:, None] & (~lower)[None, :] & ((t // (2 * w))[:, None] == (t // (2 * w))[None, :])
        masks.append(np.kron(np.eye(2), m.astype(np.float32)))
    masks.append(np.eye(2 * n, dtype=np.float32))
    tri = np.tril(np.ones((n, n), np.float32))
    rev = (t[None, :] > t[:, None]).astype(np.float32)
    blocks += [tri, rev]
    return np.concatenate(blocks, axis=0), np.stack(masks)


def _gla_kernel(x_ref, small_ref, w2_ref, bias_ref, sel_ref, mask_ref, gain_ref, o_ref, state_ref, *, chunks):
    @pl.when(pl.program_id(1) == 0)
    def _():
        state_ref[...] = jnp.zeros_like(state_ref)

    half = _half_masks()
    qk_w = GLA_HEADS * HEAD_DIM
    r = lax.broadcasted_iota(jnp.int32, (2 * LANES, LANES), 0)
    c = lax.broadcasted_iota(jnp.int32, (2 * LANES, LANES), 1)
    state_mask = (r < LANES) == (c < HEAD_DIM)
    zeros_v = jnp.zeros((CHUNK, LANES), F32)

    pairs = range(GLA_HEADS // 2)
    cps = [(ci, p) for ci in range(chunks) for p in pairs]
    crow = lambda ci: slice(ci * CHUNK, (ci + 1) * CHUNK)

    log_a = {}
    for ci in range(chunks):
        y = _dot(small_ref[crow(ci), :].astype(BF16), w2_ref[...]) + bias_ref[...]
        log_a[ci] = _log_sigmoid(y) * (1.0 / GLA_GATE_TAU)

    q2, k2, v_pair, la_split = {}, {}, {}, {}
    for ci, p in cps:
        rows = crow(ci)
        q2[ci, p] = x_ref[rows, p * LANES:(p + 1) * LANES].astype(F32) * (HEAD_DIM ** -0.5)
        k2[ci, p] = x_ref[rows, qk_w + p * LANES:qk_w + (p + 1) * LANES].astype(F32)
        v_pair[ci, p] = x_ref[rows, 2 * qk_w + 2 * p * LANES:2 * qk_w + 2 * (p + 1) * LANES]
        hi, lo = _split2(log_a[ci][:, p * LANES:(p + 1) * LANES])
        la_split[ci, p] = jnp.concatenate([hi, lo], axis=1)

    sums = {}
    for cp in cps:
        s2 = _dot(sel_ref[...], la_split[cp])
        sums[cp] = s2[:, :LANES] + s2[:, LANES:]

    a = {cp: None for cp in cps}
    for lev in range(GLA_LEVELS + 1):
        lhs, rhs = {}, {}
        for cp in cps:
            if lev < GLA_LEVELS:
                dec = jnp.exp(sums[cp][lev * CHUNK:(lev + 1) * CHUNK, :])
                ql, kl = q2[cp] * dec, k2[cp] * dec
            else:
                ql, kl = q2[cp], k2[cp]
            kb = kl.astype(BF16)
            lhs[cp] = _stack_heads(ql, half).astype(BF16)
            rhs[cp] = jnp.concatenate([kb, kb], axis=0)
        prod = {cp: _dot_nt(lhs[cp], rhs[cp]) for cp in cps}
        for cp in cps:
            term = prod[cp] * mask_ref[lev]
            a[cp] = term if a[cp] is None else a[cp] + term

    v_stack, q_dec, k_dec, b_last = {}, {}, {}, {}
    for cp in cps:
        b = sums[cp][GLA_LEVELS * CHUNK:(GLA_LEVELS + 1) * CHUNK, :]
        b_rev = sums[cp][(GLA_LEVELS + 1) * CHUNK:, :]
        vf = v_pair[cp].astype(F32)
        v_stack[cp] = jnp.concatenate(
            [jnp.concatenate([vf[:, :LANES], zeros_v], axis=1),
             jnp.concatenate([zeros_v, vf[:, LANES:]], axis=1)], axis=0).astype(BF16)
        q_dec[cp] = (q2[cp] * jnp.exp(b)).astype(BF16)
        k_dec[cp] = (k2[cp] * jnp.exp(b_rev)).astype(BF16)
        b_last[cp] = jnp.exp(b[CHUNK - 1:CHUNK, :])
    o_loc = {cp: _unstack(_dot(a[cp].astype(BF16), v_stack[cp])) for cp in cps}
    upd = {cp: jnp.where(state_mask, _dot_tn(v_pair[cp], k_dec[cp]), 0.0) for cp in cps}

    state = {p: state_ref[p] for p in pairs}
    for ci in range(chunks):
        o_pair = {p: o_loc[ci, p] + _dot_nt(q_dec[ci, p], state[p].astype(BF16)) for p in pairs}
        for p in pairs:
            state[p] = state[p] * b_last[ci, p] + upd[ci, p]
            for h in range(2):
                oh = o_pair[p][:, h * LANES:(h + 1) * LANES]
                ms = jnp.mean(oh * oh, axis=-1, keepdims=True)
                col0 = (2 * p + h) * LANES
                o_ref[crow(ci), col0:col0 + LANES] = (
                    oh * lax.rsqrt(ms + NORM_EPS) * gain_ref[...]).astype(o_ref.dtype)
    for p in pairs:
        state_ref[p] = state[p]


def _gla(big, small, w2_ext, bias_row, gain_row, batch, seq, chunks=8):
    t = batch * seq
    tm = chunks * CHUNK
    nb = seq // tm
    sel, masks = _gla_constants()
    width = 2 * GLA_HEADS * HEAD_DIM + BRANCH_WIDTH
    const2 = lambda shape: pl.BlockSpec(shape, lambda b, n: (0, 0))
    return pl.pallas_call(
        functools.partial(_gla_kernel, chunks=chunks),
        grid=(batch, nb),
        in_specs=[
            pl.BlockSpec((tm, width), lambda b, n: (b * nb + n, C_GLA * LANES // width)),
            pl.BlockSpec((tm, LANES), lambda b, n: (b * nb + n, 0)),
            const2((LANES, GLA_HEADS * HEAD_DIM)),
            const2((1, GLA_HEADS * HEAD_DIM)),
            const2(sel.shape),
            pl.BlockSpec(masks.shape, lambda b, n: (0, 0, 0)),
            const2((1, LANES)),
        ],
        out_specs=pl.BlockSpec((tm, BRANCH_WIDTH), lambda b, n: (b * nb + n, 0)),
        out_shape=jax.ShapeDtypeStruct((t, BRANCH_WIDTH), BF16),
        scratch_shapes=[pltpu.VMEM((GLA_HEADS // 2, 2 * LANES, LANES), F32)],
        compiler_params=_params(("arbitrary", "arbitrary")),
        name="gla",
    )(big, small, w2_ext, bias_row, jnp.asarray(sel, BF16), jnp.asarray(masks, F32), gain_row)


def _merge_kernel(x_ref, o0_ref, o1_ref, o2_ref, o3_ref, z_ref, ml_ref, gb_ref, wb_ref, wo_ref, out_ref):
    merged = None
    for n, o_ref in enumerate((o0_ref, o1_ref, o2_ref, o3_ref)):
        zs = z_ref[:, n * BRANCH_WIDTH:(n + 1) * BRANCH_WIDTH].astype(F32)
        br = (o_ref[...].astype(F32) * (zs * (1.0 / NEG_LOG2E)) / (1.0 + jnp.exp2(zs))).astype(BF16)
        proj = _dot(br, wb_ref[n])
        gate = 1.0 / (1.0 + jnp.exp2(ml_ref[:, n * D_MODEL:(n + 1) * D_MODEL].astype(F32)
                                     + gb_ref[:, n * D_MODEL:(n + 1) * D_MODEL]))
        term = proj * gate
        merged = term if merged is None else merged + term
    out_ref[...] = x_ref[...] + _dot(merged.astype(BF16), wo_ref[...])


def _merge(x2, outs, big, gate_bias_row, w_branch, w_out, tm=256):
    t = x2.shape[0]
    zw = N_BRANCH * BRANCH_WIDTH
    mw = N_BRANCH * D_MODEL
    o_spec = pl.BlockSpec((tm, BRANCH_WIDTH), lambda i: (i, 0))
    return pl.pallas_call(
        _merge_kernel,
        grid=(t // tm,),
        in_specs=[
            pl.BlockSpec((tm, D_MODEL), lambda i: (i, 0)),
            o_spec, o_spec, o_spec, o_spec,
            pl.BlockSpec((tm, zw), lambda i: (i, C_Z * LANES // zw)),
            pl.BlockSpec((tm, mw), lambda i: (i, C_MERGE * LANES // mw)),
            pl.BlockSpec((1, mw), lambda i: (0, 0)),
            pl.BlockSpec((N_BRANCH, BRANCH_WIDTH, D_MODEL), lambda i: (0, 0, 0), pipeline_mode=pl.Buffered(1)),
            pl.BlockSpec((D_MODEL, D_MODEL), lambda i: (0, 0), pipeline_mode=pl.Buffered(1)),
        ],
        out_specs=pl.BlockSpec((tm, D_MODEL), lambda i: (i, 0)),
        out_shape=jax.ShapeDtypeStruct((t, D_MODEL), F32),
        compiler_params=_params(("arbitrary",)),
        name="merge_out",
    )(x2, *outs, big, big, gate_bias_row, w_branch, w_out)


def _reorder_w_in(w_in):
    sizes = (3 * BRANCH_WIDTH, N_HEADS, BRANCH_WIDTH, 3 * BRANCH_WIDTH, BRANCH_WIDTH,
             3 * BRANCH_WIDTH, N_HEADS, N_HEADS, BRANCH_WIDTH,
             2 * GLA_HEADS * HEAD_DIM, BRANCH_WIDTH, GLA_GATE_RANK, BRANCH_WIDTH, N_BRANCH * D_MODEL)
    (fox_qkv, fox_f, fox_z, sb_qkv, sb_z, gdn_qkv, gdn_a, gdn_b, gdn_z,
     gla_qk, gla_v, gla_g, gla_z, merge) = jnp.split(w_in, np.cumsum(sizes)[:-1].tolist(), axis=1)
    gate_cols = jnp.concatenate([merge, fox_z, sb_z, gdn_z, gla_z], axis=1) * NEG_LOG2E
    big = jnp.concatenate([gate_cols, gla_qk, gla_v, gdn_qkv, fox_qkv, sb_qkv], axis=1)
    pad = jnp.zeros((w_in.shape[0], LANES - (3 * N_HEADS + GLA_GATE_RANK)), w_in.dtype)
    small = jnp.concatenate([fox_f, gdn_a, gdn_b, gla_g, pad], axis=1)
    return big.astype(BF16), small.astype(BF16)


def _lane_row(values, offset):
    row = jnp.zeros((1, LANES), F32)
    return row.at[0, offset:offset + values.shape[0]].set(values.astype(F32))


def _layer(x2, batch, seq, norm_g, w_in, fox_f_bias, fox_q_norm, fox_k_norm, sb_q_norm, sb_k_norm,
           gdn_conv_w, gdn_a_log, gdn_dt_bias, gdn_o_norm, gla_gate_w2, gla_gate_bias, gla_o_norm,
           gate_bias, w_branch, w_out):
    w_big, w_small = _reorder_w_in(w_in)
    g_row = norm_g.reshape(1, D_MODEL).astype(F32)
    t = batch * seq
    big, small = _norm_proj(x2, g_row, w_big, w_small, min(2048, t), 512)

    pair = lambda g: jnp.tile(g.astype(F32), 2).reshape(1, LANES)

    c = _fox_cum(small, _lane_row(fox_f_bias, L_FOXF), batch, seq)
    tq = tk = min(512, seq)
    c3 = c.reshape(batch, seq, LANES)[:, :, :N_HEADS]
    first = jnp.transpose(c3[:, 0::tq, :], (0, 2, 1))
    last = jnp.transpose(c3[:, tk - 1::tk, :], (0, 2, 1))
    logit_bound = lambda gq, gk: 1.02 * HEAD_DIM ** 0.5 * jnp.max(jnp.abs(gq)) * jnp.max(jnp.abs(gk))
    qk_bound = logit_bound(fox_q_norm, fox_k_norm)
    thr = EXP_UNDERFLOW + 2.0 * qk_bound + 1.0
    tab = jnp.concatenate([first.reshape(-1), last.reshape(-1), thr.reshape(1), qk_bound.reshape(1)]).astype(F32)
    o_fox = _fox_attention(big, c, tab, pair(fox_q_norm), pair(fox_k_norm), batch, seq, tq, tk)

    o_sb = _sb_attention(big, logit_bound(sb_q_norm, sb_k_norm).reshape(1).astype(F32),
                         pair(sb_q_norm), pair(sb_k_norm), batch, seq, min(256, seq))

    o_gdn = _gdn(big, small, gdn_conv_w.astype(F32), _lane_row(gdn_a_log, L_GDNA),
                 _lane_row(gdn_dt_bias, L_GDNA), pair(gdn_o_norm), batch, seq)

    w2_ext = jnp.zeros((LANES, GLA_HEADS * HEAD_DIM), F32).at[L_GLAG:L_GLAG + GLA_GATE_RANK].set(
        gla_gate_w2.astype(F32)).astype(BF16)
    o_gla = _gla(big, small, w2_ext, gla_gate_bias.reshape(1, -1).astype(F32),
                 gla_o_norm.reshape(1, LANES).astype(F32), batch, seq)

    return _merge(x2, (o_fox, o_sb, o_gdn, o_gla), big, gate_bias.reshape(1, -1).astype(F32) * NEG_LOG2E,
                  w_branch.astype(BF16), w_out.astype(BF16), min(512, t))


def kernel(x, norm_g, w_in, fox_f_bias, fox_q_norm, fox_k_norm, sb_q_norm, sb_k_norm, gdn_conv_w,
           gdn_a_log, gdn_dt_bias, gdn_o_norm, gla_gate_w2, gla_gate_bias, gla_o_norm, gate_bias,
           w_branch, w_out):
    batch, seq, d = x.shape
    x2 = x.reshape(batch * seq, d)
    params = (norm_g, w_in, fox_f_bias, fox_q_norm, fox_k_norm, sb_q_norm, sb_k_norm, gdn_conv_w,
              gdn_a_log, gdn_dt_bias, gdn_o_norm, gla_gate_w2, gla_gate_bias, gla_o_norm, gate_bias,
              w_branch, w_out)
    for l in range(norm_g.shape[0]):
        x2 = _layer(x2, batch, seq, *[p[l] for p in params])
    return x2.reshape(batch, seq, d)
```
